```python
import math
import jax, jax.numpy as jnp
from jax import lax
import numpy as np

D_MODEL = 1024
BATCH = 32
SEQ = 2048
DEPTH = 2

CHUNK = 64
HEAD_DIM = 64
A_HEADS = 4
A_LEFT_CHUNKS = 8
A_BAND = A_LEFT_CHUNKS + 1
REL_CLIP = 128
B_HEADS = 4
B_V_DIM = 2 * HEAD_DIM
B_QBLOCK = 128
C_HEADS = 4
C_TOPK_MAX = 256
C_QBLOCK = 32
IDX_HEADS = 8
IDX_DIM = 64
A_WIDTH = A_HEADS * HEAD_DIM
B_WIDTH = B_HEADS * B_V_DIM
C_WIDTH = C_HEADS * HEAD_DIM
MIX_WIDTH = A_WIDTH + B_WIDTH + C_WIDTH
IN_SIZES = (A_WIDTH, A_WIDTH, A_WIDTH,
            2 * B_HEADS * HEAD_DIM, 2 * B_HEADS * HEAD_DIM, B_WIDTH,
            C_WIDTH, C_WIDTH, C_WIDTH,
            IDX_HEADS * IDX_DIM, IDX_DIM, IDX_HEADS)
IN_TOTAL = sum(IN_SIZES)
D_FF = 4 * D_MODEL
PLE_DIM = 256
ROPE_THETA = 10000.0
DEEPNORM_ALPHA = (2 * DEPTH) ** 0.25
DEEPNORM_BETA = (8 * DEPTH) ** -0.25
NORM_EPS = 1e-5
NEG_INF = -1e30

kernel_name = 'hybrid_chunk_causal_encoder_block'


def layer_norm(x, g, b):
    xf = x.astype(jnp.float32)
    mu = jnp.mean(xf, axis=-1, keepdims=True)
    var = jnp.mean(jnp.square(xf - mu), axis=-1, keepdims=True)
    y = (xf - mu) * lax.rsqrt(var + NORM_EPS) * g.astype(jnp.float32) + b.astype(jnp.float32)
    return y.astype(x.dtype)


def rms_norm(x, g):
    xf = x.astype(jnp.float32)
    y = xf * lax.rsqrt(jnp.mean(jnp.square(xf), axis=-1, keepdims=True) + NORM_EPS)
    return (y * g.astype(jnp.float32)).astype(x.dtype)


def rope_tables(positions, dim):
    inv = ROPE_THETA ** (-jnp.arange(0, dim, 2, dtype=jnp.float32) / dim)
    ang = positions.astype(jnp.float32)[:, :, None] * inv
    return jnp.cos(ang)[:, :, None, :], jnp.sin(ang)[:, :, None, :]


def apply_rope(x, cos, sin):
    xf = x.astype(jnp.float32)
    x1, x2 = jnp.split(xf, 2, axis=-1)
    return jnp.concatenate([x1 * cos - x2 * sin, x2 * cos + x1 * sin], axis=-1).astype(x.dtype)


def to_blocks(x, blk):
    b, s = x.shape[:2]
    return jnp.moveaxis(x.reshape(b, s // blk, blk, *x.shape[2:]), 1, 0)


def from_blocks(x):
    nb, b, blk = x.shape[:3]
    return jnp.moveaxis(x, 0, 1).reshape(b, nb * blk, *x.shape[3:])


def chunk_band_attention(q, k, v, rel_bias):
    bsz, seq, nh, d = q.shape
    pad = A_LEFT_CHUNKS * CHUNK
    band_len = A_BAND * CHUNK
    k_pad = jnp.pad(k, ((0, 0), (pad, 0), (0, 0), (0, 0)))
    v_pad = jnp.pad(v, ((0, 0), (pad, 0), (0, 0), (0, 0)))
    qi = jnp.arange(CHUNK)[:, None]
    kj = jnp.arange(band_len)[None, :]
    rel = jnp.clip(pad + qi - kj, -REL_CLIP, REL_CLIP) + REL_CLIP
    bias = rel_bias[:, rel].astype(jnp.float32)
    scale = d ** -0.5

    def one_chunk(args):
        c, qb = args
        start = c * CHUNK
        kb = lax.dynamic_slice_in_dim(k_pad, start, band_len, axis=1)
        vb = lax.dynamic_slice_in_dim(v_pad, start, band_len, axis=1)
        s = jnp.einsum('bqhd,bkhd->bhqk', qb, kb).astype(jnp.float32) * scale + bias
        valid = (start - pad + jnp.arange(band_len)) >= 0
        s = jnp.where(valid, s, NEG_INF)
        pr = jax.nn.softmax(s, axis=-1).astype(vb.dtype)
        return jnp.einsum('bhqk,bkhd->bqhd', pr, vb)

    out = lax.map(one_chunk, (jnp.arange(seq // CHUNK), to_blocks(q, CHUNK)))
    return from_blocks(out).reshape(bsz, seq, nh * d)


def diff_attention(q1, q2, k1, k2, v, lam):
    seq, d = q1.shape[1], q1.shape[-1]
    key_chunk = jnp.arange(seq) // CHUNK
    scale = d ** -0.5

    def one_block(args):
        blk, q1b, q2b = args
        q_chunk = (blk * B_QBLOCK + jnp.arange(B_QBLOCK)) // CHUNK
        mask = key_chunk[None, :] <= q_chunk[:, None]
        s1 = jnp.einsum('bqhd,bkhd->bhqk', q1b, k1).astype(jnp.float32) * scale
        s2 = jnp.einsum('bqhd,bkhd->bhqk', q2b, k2).astype(jnp.float32) * scale
        a = (jax.nn.softmax(jnp.where(mask, s1, NEG_INF), axis=-1)
             - lam * jax.nn.softmax(jnp.where(mask, s2, NEG_INF), axis=-1))
        return jnp.einsum('bhqk,bkhe->bqhe', a.astype(v.dtype), v)

    out = lax.map(one_block, (jnp.arange(seq // B_QBLOCK), to_blocks(q1, B_QBLOCK), to_blocks(q2, B_QBLOCK)))
    return from_blocks(out)


def dsa_attention(q, k, v, iq, ik, iw):
    bsz, seq, nh, d = q.shape
    topk = min(C_TOPK_MAX, seq // 4)
    key_chunk = jnp.arange(seq) // CHUNK
    scale = d ** -0.5
    gather = jax.vmap(lambda t, idx: t[idx])

    def one_block(args):
        blk, qb, iqb, iwb = args
        q_chunk = (blk * C_QBLOCK + jnp.arange(C_QBLOCK)) // CHUNK
        logits = jax.nn.relu(jnp.einsum('bqhd,bsd->bqhs', iqb, ik).astype(jnp.float32))
        score = jnp.einsum('bqhs,bqh->bqs', logits, iwb.astype(jnp.float32))
        admissible = key_chunk[None, :] <= q_chunk[:, None]
        score = jnp.where(admissible[None], score, NEG_INF)
        _, idx = lax.top_k(score, topk)
        k_sel = gather(k, idx)
        v_sel = gather(v, idx)
        valid = key_chunk[idx] <= q_chunk[None, :, None]
        s = jnp.einsum('bqhd,bqkhd->bhqk', qb, k_sel).astype(jnp.float32) * scale
        s = jnp.where(valid[:, None], s, NEG_INF)
        pr = jax.nn.softmax(s, axis=-1).astype(v.dtype)
        return jnp.einsum('bhqk,bqkhd->bqhd', pr, v_sel)

    out = lax.map(one_block, (jnp.arange(seq // C_QBLOCK), to_blocks(q, C_QBLOCK),
                              to_blocks(iq, C_QBLOCK), to_blocks(iw, C_QBLOCK)))
    return from_blocks(out).reshape(bsz, seq, nh * d)


def setup_inputs(seed: int = 0) -> dict:
    key = jax.random.key(seed)
    ks = jax.random.split(key, 19)

    def nrm(k, shape, scale):
        return jax.random.normal(k, shape, jnp.float32) * scale

    x = nrm(ks[0], (BATCH, SEQ, D_MODEL), 1.0)
    p = nrm(ks[1], (DEPTH, BATCH, SEQ, PLE_DIM), 1.0)
    offset = jax.random.randint(ks[2], (BATCH, 1), 0, 4096, dtype=jnp.int32)
    positions = offset + jnp.arange(SEQ, dtype=jnp.int32)[None, :]
    w_in = nrm(ks[3], (DEPTH, D_MODEL, IN_TOTAL), D_MODEL ** -0.5)
    rel_bias = nrm(ks[4], (DEPTH, A_HEADS, 2 * REL_CLIP + 1), 0.5)
    lam_q1 = nrm(ks[5], (DEPTH, HEAD_DIM), 0.1)
    lam_k1 = nrm(ks[6], (DEPTH, HEAD_DIM), 0.1)
    lam_q2 = nrm(ks[7], (DEPTH, HEAD_DIM), 0.1)
    lam_k2 = nrm(ks[8], (DEPTH, HEAD_DIM), 0.1)
    diff_norm_g = 1.0 + nrm(ks[9], (DEPTH, B_V_DIM), 0.05)
    w_o = nrm(ks[10], (DEPTH, MIX_WIDTH, D_MODEL), DEEPNORM_BETA * MIX_WIDTH ** -0.5)
    ln1_g = 1.0 + nrm(ks[11], (DEPTH, D_MODEL), 0.05)
    ln1_b = nrm(ks[12], (DEPTH, D_MODEL), 0.02)
    w_up = nrm(ks[13], (DEPTH, D_MODEL, D_FF), D_MODEL ** -0.5)
    w_down = nrm(ks[14], (DEPTH, D_FF, D_MODEL), DEEPNORM_BETA * D_FF ** -0.5)
    w_ple_gate = nrm(ks[15], (DEPTH, D_MODEL, D_MODEL), D_MODEL ** -0.5)
    w_ple = nrm(ks[16], (DEPTH, PLE_DIM, D_MODEL), DEEPNORM_BETA * PLE_DIM ** -0.5)
    ln2_g = 1.0 + nrm(ks[17], (DEPTH, D_MODEL), 0.05)
    ln2_b = nrm(ks[18], (DEPTH, D_MODEL), 0.02)
    return {'x': x, 'p': p, 'positions': positions, 'w_in': w_in, 'rel_bias': rel_bias,
            'lam_q1': lam_q1, 'lam_k1': lam_k1, 'lam_q2': lam_q2, 'lam_k2': lam_k2,
            'diff_norm_g': diff_norm_g, 'w_o': w_o, 'ln1_g': ln1_g, 'ln1_b': ln1_b,
            'w_up': w_up, 'w_down': w_down, 'w_ple_gate': w_ple_gate, 'w_ple': w_ple,
            'ln2_g': ln2_g, 'ln2_b': ln2_b}


def reference(x, p, positions, w_in, rel_bias, lam_q1, lam_k1, lam_q2, lam_k2,
              diff_norm_g, w_o, ln1_g, ln1_b, w_up, w_down, w_ple_gate, w_ple,
              ln2_g, ln2_b):
    bsz, seq = x.shape[:2]
    cos, sin = rope_tables(positions, HEAD_DIM)
    split_at = np.cumsum(IN_SIZES)[:-1].tolist()

    def heads(t, n, d):
        return t.reshape(bsz, seq, n, d)

    for i in range(DEPTH):
        h = jnp.einsum('bsd,de->bse', x, w_in[i])
        aq, ak, av, bq, bk, bv, cq, ck, cv, iq, ik, iw = jnp.split(h, split_at, axis=-1)

        o_a = chunk_band_attention(heads(aq, A_HEADS, HEAD_DIM), heads(ak, A_HEADS, HEAD_DIM),
                                   heads(av, A_HEADS, HEAD_DIM), rel_bias[i])

        bq = apply_rope(heads(bq, 2 * B_HEADS, HEAD_DIM), cos, sin).reshape(bsz, seq, B_HEADS, 2, HEAD_DIM)
        bk = apply_rope(heads(bk, 2 * B_HEADS, HEAD_DIM), cos, sin).reshape(bsz, seq, B_HEADS, 2, HEAD_DIM)
        lam_init = 0.8 - 0.6 * math.exp(-0.3 * i)
        lam = (jnp.exp(jnp.sum(lam_q1[i].astype(jnp.float32) * lam_k1[i].astype(jnp.float32)))
               - jnp.exp(jnp.sum(lam_q2[i].astype(jnp.float32) * lam_k2[i].astype(jnp.float32)))
               + lam_init)
        o_b = diff_attention(bq[:, :, :, 0], bq[:, :, :, 1], bk[:, :, :, 0], bk[:, :, :, 1],
                             heads(bv, B_HEADS, B_V_DIM), lam)
        o_b = (rms_norm(o_b, diff_norm_g[i]) * (1.0 - lam_init)).reshape(bsz, seq, B_WIDTH)

        o_c = dsa_attention(apply_rope(heads(cq, C_HEADS, HEAD_DIM), cos, sin),
                            apply_rope(heads(ck, C_HEADS, HEAD_DIM), cos, sin),
                            heads(cv, C_HEADS, HEAD_DIM),
                            apply_rope(heads(iq, IDX_HEADS, IDX_DIM), cos, sin),
                            apply_rope(heads(ik, 1, IDX_DIM), cos, sin)[:, :, 0],
                            iw * (IDX_HEADS * IDX_DIM) ** -0.5)

        mix = jnp.concatenate([o_a, o_b, o_c], axis=-1)
        x = layer_norm(DEEPNORM_ALPHA * x + jnp.einsum('bse,ed->bsd', mix, w_o[i]), ln1_g[i], ln1_b[i])

        ff = jnp.einsum('bsf,fd->bsd', jnp.square(jax.nn.relu(jnp.einsum('bsd,df->bsf', x, w_up[i]))), w_down[i])
        ple = jax.nn.sigmoid(jnp.einsum('bsd,de->bse', x, w_ple_gate[i])) * jnp.einsum('bsr,rd->bsd', p[i], w_ple[i])
        x = layer_norm(DEEPNORM_ALPHA * x + ff + ple, ln2_g[i], ln2_b[i])
    return x
```

```python
import functools
import math

import jax
import jax.numpy as jnp
import numpy as np
from jax import lax
from jax.experimental import pallas as pl
from jax.experimental.pallas import tpu as pltpu

CHUNK = 64
HEAD_DIM = 64
A_HEADS = 4
A_LEFT_CHUNKS = 8
REL_CLIP = 128
B_HEADS = 4
B_V_DIM = 2 * HEAD_DIM
C_HEADS = 4
C_TOPK_MAX = 256
IDX_HEADS = 8
IDX_DIM = 64
A_WIDTH = A_HEADS * HEAD_DIM
B_WIDTH = B_HEADS * B_V_DIM
C_WIDTH = C_HEADS * HEAD_DIM
ROPE_THETA = 10000.0
NORM_EPS = 1e-5
NEG_INF = -1e30

LANES = 128
Q_BLOCK = 128
VMEM_LIMIT = 56 * 1024 * 1024

OFF_A = 0
OFF_BQ = 3 * A_WIDTH
OFF_BK = OFF_BQ + 2 * B_HEADS * HEAD_DIM
OFF_BV = OFF_BK + 2 * B_HEADS * HEAD_DIM
OFF_CQ = OFF_BV + B_WIDTH
OFF_CK = OFF_CQ + C_WIDTH
OFF_CV = OFF_CK + C_WIDTH
OFF_IQ = OFF_CV + C_WIDTH
OFF_IK = OFF_IQ + IDX_HEADS * IDX_DIM
IN_TOTAL = OFF_IK + IDX_DIM + IDX_HEADS
IN_PADDED = OFF_IK + LANES

_MXU_DTYPE = jnp.bfloat16
_NT = (((1,), (1,)), ((), ()))


def _params(sem):
    return pltpu.CompilerParams(dimension_semantics=sem, vmem_limit_bytes=VMEM_LIMIT)


def _resident(shape):
    return pl.BlockSpec(shape, lambda *_: (0,) * len(shape), pipeline_mode=pl.Buffered(1))


def _rope_table_kernel(pos_ref, inv_ref, sgn_ref, cos_ref, sin_ref):
    ang = pos_ref[...].astype(jnp.float32) * inv_ref[...]
    cos_ref[...] = jnp.cos(ang)
    sin_ref[...] = jnp.sin(ang) * sgn_ref[...]


def _rope_tables(positions):
    m = positions.size
    tm = min(m, 2048)
    inv = ROPE_THETA ** (-jnp.arange(0, HEAD_DIM, 2, dtype=jnp.float32) / HEAD_DIM)
    inv = jnp.tile(inv, LANES // (HEAD_DIM // 2))[None, :]
    sgn = jnp.tile(jnp.concatenate([-jnp.ones(HEAD_DIM // 2), jnp.ones(HEAD_DIM // 2)]),
                   LANES // HEAD_DIM).astype(jnp.float32)[None, :]
    row = pl.BlockSpec((tm, LANES), lambda i: (i, 0))
    const = pl.BlockSpec((1, LANES), lambda i: (0, 0))
    return pl.pallas_call(
        _rope_table_kernel,
        grid=(m // tm,),
        in_specs=[pl.BlockSpec((tm, 1), lambda i: (i, 0)), const, const],
        out_specs=[row, row],
        out_shape=[jax.ShapeDtypeStruct((m, LANES), jnp.float32)] * 2,
        compiler_params=_params(("parallel",)),
    )(positions.reshape(m, 1), inv, sgn)


def _in_proj_kernel(x_ref, w_ref, cos_ref, sin_ref,
                    a_ref, bq_ref, bk_ref, bv_ref, cq_ref, ck_ref, cvt_ref,
                    iq_ref, ik2_ref, iwt_ref, *, key_tile):
    tm = x_ref.shape[0]
    xb = x_ref[...].astype(_MXU_DTYPE)
    cos = cos_ref[...]
    sin = sin_ref[...]
    lane = lax.broadcasted_iota(jnp.int32, (tm, LANES), 1)
    low_half = (lane & (HEAD_DIM - 1)) < HEAD_DIM // 2
    qk_scale = HEAD_DIM ** -0.5

    def proj(c0, n):
        return jnp.dot(xb, w_ref[:, c0:c0 + n], preferred_element_type=jnp.float32)

    def rope(t):
        rot = jnp.where(low_half, pltpu.roll(t, LANES - HEAD_DIM // 2, 1),
                        pltpu.roll(t, HEAD_DIM // 2, 1))
        return t * cos + rot * sin

    def store(ref, h, roped=False, scale=None, col0=0):
        for j in range(h.shape[1] // LANES):
            t = h[:, j * LANES:(j + 1) * LANES]
            if roped:
                t = rope(t)
            if scale is not None:
                t = t * scale
            ref[:, col0 + j * LANES:col0 + (j + 1) * LANES] = t.astype(ref.dtype)

    store(a_ref, proj(OFF_A, A_WIDTH), scale=qk_scale)
    store(a_ref, proj(OFF_A + A_WIDTH, 2 * A_WIDTH), col0=A_WIDTH)
    store(bq_ref, proj(OFF_BQ, OFF_BK - OFF_BQ), roped=True, scale=qk_scale)
    store(bk_ref, proj(OFF_BK, OFF_BV - OFF_BK), roped=True)
    store(bv_ref, proj(OFF_BV, B_WIDTH))
    store(cq_ref, proj(OFF_CQ, C_WIDTH), roped=True, scale=qk_scale)
    store(ck_ref, proj(OFF_CK, C_WIDTH), roped=True)
    cv = proj(OFF_CV, C_WIDTH)
    for c in range(tm // key_tile):
        cvt_ref[c] = cv[c * key_tile:(c + 1) * key_tile, :].T.astype(cvt_ref.dtype)
    store(iq_ref, proj(OFF_IQ, IDX_HEADS * IDX_DIM), roped=True)
    last = proj(OFF_IK, LANES)
    ikr = rope(last)
    ik2 = jnp.where(lane < IDX_DIM, ikr, pltpu.roll(ikr, IDX_DIM, 1))
    ik2_ref[...] = ik2.astype(ik2_ref.dtype)
    iwt_ref[...] = last.T[IDX_DIM:IDX_DIM + IDX_HEADS, :] * ((IDX_HEADS * IDX_DIM) ** -0.5)


def _in_proj(x, w, cos, sin, bsz, seq, key_tile):
    m, d = x.shape
    tm = min(seq, 512)
    n_s = seq // tm
    f = _MXU_DTYPE

    def rows(width):
        return pl.BlockSpec((tm, width), lambda i: (i, 0))

    out_shape = [
        jax.ShapeDtypeStruct((m, 3 * A_WIDTH), f),
        jax.ShapeDtypeStruct((m, OFF_BK - OFF_BQ), f),
        jax.ShapeDtypeStruct((m, OFF_BV - OFF_BK), f),
        jax.ShapeDtypeStruct((m, B_WIDTH), f),
        jax.ShapeDtypeStruct((m, C_WIDTH), f),
        jax.ShapeDtypeStruct((m, C_WIDTH), f),
        jax.ShapeDtypeStruct((bsz, seq // key_tile, C_WIDTH, key_tile), f),
        jax.ShapeDtypeStruct((m, IDX_HEADS * IDX_DIM), f),
        jax.ShapeDtypeStruct((m, LANES), f),
        jax.ShapeDtypeStruct((bsz, IDX_HEADS, seq), jnp.float32),
    ]
    out_specs = [
        rows(3 * A_WIDTH), rows(OFF_BK - OFF_BQ), rows(OFF_BV - OFF_BK), rows(B_WIDTH),
        rows(C_WIDTH), rows(C_WIDTH),
        pl.BlockSpec((None, tm // key_tile, C_WIDTH, key_tile), lambda i: (i // n_s, i % n_s, 0, 0)),
        rows(IDX_HEADS * IDX_DIM), rows(LANES),
        pl.BlockSpec((None, IDX_HEADS, tm), lambda i: (i // n_s, 0, i % n_s)),
    ]
    return pl.pallas_call(
        functools.partial(_in_proj_kernel, key_tile=key_tile),
        grid=(m // tm,),
        in_specs=[rows(d), _resident(w.shape), rows(LANES), rows(LANES)],
        out_specs=out_specs,
        out_shape=out_shape,
        compiler_params=_params(("parallel",)),
    )(x, w, cos, sin)


A_WINDOW_TILES = (A_LEFT_CHUNKS * CHUNK) // Q_BLOCK + 1


def _band_bias(rel_bias):
    width = A_WINDOW_TILES * Q_BLOCK
    r = np.arange(Q_BLOCK)[:, None]
    e = np.arange(width)[None, :]
    rel = np.clip(r - e + A_LEFT_CHUNKS * CHUNK, -REL_CLIP, REL_CLIP) + REL_CLIP
    back = r // CHUNK + A_LEFT_CHUNKS - e // CHUNK
    in_band = (back >= 0) & (back <= A_LEFT_CHUNKS)
    bias = rel_bias[:, rel].astype(jnp.float32)
    return jnp.where(jnp.asarray(in_band)[None], bias, NEG_INF)


def _band_kernel(q_ref, k_ref, v_ref, bias_ref, o_ref):
    i = pl.program_id(1)
    lane = lax.broadcasted_iota(jnp.int32, (Q_BLOCK, LANES), 1)
    even_head = lane < HEAD_DIM
    zero = jnp.zeros((), q_ref.dtype)
    for pair in range(A_HEADS // 2):
        cols = slice(pair * LANES, (pair + 1) * LANES)
        qt = q_ref[:, cols]
        qs = jnp.concatenate([jnp.where(even_head, qt, zero), jnp.where(even_head, zero, qt)], axis=0)
        s_tiles = []
        for tt in range(A_WINDOW_TILES):
            t = i - (A_WINDOW_TILES - 1) + tt
            k0 = pl.multiple_of(jnp.maximum(t, 0) * Q_BLOCK, Q_BLOCK)
            kt = k_ref[pl.ds(k0, Q_BLOCK), cols]
            s = lax.dot_general(qs, kt, _NT, preferred_element_type=jnp.float32)
            lanes_tt = slice(tt * Q_BLOCK, (tt + 1) * Q_BLOCK)
            b = jnp.concatenate([bias_ref[2 * pair, :, lanes_tt], bias_ref[2 * pair + 1, :, lanes_tt]], axis=0)
            s_tiles.append(jnp.where(t >= 0, s + b, NEG_INF))
        m = functools.reduce(jnp.maximum, [jnp.max(s, axis=1, keepdims=True) for s in s_tiles])
        p_tiles = [jnp.exp(s - m) for s in s_tiles]
        l = functools.reduce(jnp.add, [jnp.sum(p, axis=1, keepdims=True) for p in p_tiles])
        acc = jnp.zeros((2 * Q_BLOCK, LANES), jnp.float32)
        for tt in range(A_WINDOW_TILES):
            t = i - (A_WINDOW_TILES - 1) + tt
            k0 = pl.multiple_of(jnp.maximum(t, 0) * Q_BLOCK, Q_BLOCK)
            vt = v_ref[pl.ds(k0, Q_BLOCK), cols]
            acc = acc + jnp.dot(p_tiles[tt].astype(_MXU_DTYPE), vt, preferred_element_type=jnp.float32)
        acc = acc * (1.0 / l)
        o_ref[:, cols] = jnp.where(even_head, acc[:Q_BLOCK], acc[Q_BLOCK:]).astype(o_ref.dtype)


def _band_attention(a_qkv, bias, bsz, seq):
    m = a_qkv.shape[0]
    nq = seq // Q_BLOCK
    return pl.pallas_call(
        _band_kernel,
        grid=(bsz, nq),
        in_specs=[
            pl.BlockSpec((Q_BLOCK, A_WIDTH), lambda b, i: (b * nq + i, 0)),
            pl.BlockSpec((seq, A_WIDTH), lambda b, i: (b, 1)),
            pl.BlockSpec((seq, A_WIDTH), lambda b, i: (b, 2)),
            _resident(bias.shape),
        ],
        out_specs=pl.BlockSpec((Q_BLOCK, A_WIDTH), lambda b, i: (b * nq + i, 0)),
        out_shape=jax.ShapeDtypeStruct((m, A_WIDTH), _MXU_DTYPE),
        compiler_params=_params(("parallel", "arbitrary")),
    )(a_qkv, a_qkv, a_qkv, bias)


def _diff_kernel(lam_ref, g_ref, q_ref, k_ref, v_ref, o_ref, s_ref, *, lam_init, key_tile):
    i = pl.program_id(2)
    lv = lam_ref[...]
    lam = (jnp.exp(jnp.sum(lv[0:1] * lv[1:2], axis=1, keepdims=True))
           - jnp.exp(jnp.sum(lv[2:3] * lv[3:4], axis=1, keepdims=True)) + lam_init)
    lane = lax.broadcasted_iota(jnp.int32, (Q_BLOCK, LANES), 1)
    first_map = lane < HEAD_DIM
    zero = jnp.zeros((), q_ref.dtype)
    qt = q_ref[...]
    qs = jnp.concatenate([jnp.where(first_map, qt, zero), jnp.where(first_map, zero, qt)], axis=0)
    row = lax.broadcasted_iota(jnp.int32, (2 * Q_BLOCK, 1), 0)
    limit = jnp.where((row & (Q_BLOCK - 1)) < CHUNK, i * Q_BLOCK + CHUNK, (i + 1) * Q_BLOCK)
    n_tiles = ((i + 1) * Q_BLOCK + key_tile - 1) // key_tile

    def scores(t, m):
        k0 = pl.multiple_of(t * key_tile, key_tile)
        kt = k_ref[pl.ds(k0, key_tile), :]
        s = lax.dot_general(qs, kt, _NT, preferred_element_type=jnp.float32)
        kpos = k0 + lax.broadcasted_iota(jnp.int32, s.shape, 1)
        s = jnp.where(kpos < limit, s, NEG_INF)
        s_ref[t] = s
        return jnp.maximum(m, jnp.max(s, axis=1, keepdims=True))

    m = lax.fori_loop(0, n_tiles, scores, jnp.full((2 * Q_BLOCK, 1), NEG_INF, jnp.float32))

    def exps(t, l):
        p = jnp.exp(s_ref[t] - m)
        s_ref[t] = p
        return l + jnp.sum(p, axis=1, keepdims=True)

    l = lax.fori_loop(0, n_tiles, exps, jnp.zeros((2 * Q_BLOCK, 1), jnp.float32))
    r = 1.0 / l
    r1 = r[:Q_BLOCK]
    r2 = r[Q_BLOCK:] * lam

    def weighted(t, acc):
        k0 = pl.multiple_of(t * key_tile, key_tile)
        p = s_ref[t]
        a = p[:Q_BLOCK] * r1 - p[Q_BLOCK:] * r2
        vt = v_ref[pl.ds(k0, key_tile), :]
        return acc + jnp.dot(a.astype(_MXU_DTYPE), vt, preferred_element_type=jnp.float32)

    o = lax.fori_loop(0, n_tiles, weighted, jnp.zeros((Q_BLOCK, B_V_DIM), jnp.float32))
    ms = jnp.mean(o * o, axis=1, keepdims=True)
    o = o * lax.rsqrt(ms + NORM_EPS) * g_ref[...] * (1.0 - lam_init)
    o_ref[...] = o.astype(o_ref.dtype)


def _diff_attention(bq, bk, bv, lam_vecs, gain, lam_init, bsz, seq):
    m = bq.shape[0]
    nq = seq // Q_BLOCK
    key_tile = min(seq, 512)
    return pl.pallas_call(
        functools.partial(_diff_kernel, lam_init=lam_init, key_tile=key_tile),
        grid=(bsz, B_HEADS, nq),
        in_specs=[
            _resident(lam_vecs.shape),
            _resident(gain.shape),
            pl.BlockSpec((Q_BLOCK, LANES), lambda b, h, i: (b * nq + i, h)),
            pl.BlockSpec((seq, LANES), lambda b, h, i: (b, h)),
            pl.BlockSpec((seq, B_V_DIM), lambda b, h, i: (b, h)),
        ],
        out_specs=pl.BlockSpec((Q_BLOCK, B_V_DIM), lambda b, h, i: (b * nq + i, h)),
        out_shape=jax.ShapeDtypeStruct((m, B_WIDTH), _MXU_DTYPE),
        scratch_shapes=[pltpu.VMEM((seq // key_tile, 2 * Q_BLOCK, key_tile), jnp.float32)],
        compiler_params=_params(("parallel", "parallel", "arbitrary")),
    )(lam_vecs, gain, bq, bk, bv)


C_BISECT_STEPS = 16


def _dsa_kernel(ik_ref, iq_ref, iw_ref, ck_ref, cvt_ref, cq_ref, o_ref, sc_ref, s_ref,
                *, topk, key_tile, seq):
    i = pl.program_id(1)
    tk = key_tile
    n_tiles = ((i + 1) * Q_BLOCK + tk - 1) // tk
    lane_q = lax.broadcasted_iota(jnp.int32, (1, LANES), 1)
    limit = jnp.where(lane_q < CHUNK, i * Q_BLOCK + CHUNK, (i + 1) * Q_BLOCK)
    lane = lax.broadcasted_iota(jnp.int32, (Q_BLOCK, LANES), 1)
    even_head = lane < HEAD_DIM
    zero = jnp.zeros((), iq_ref.dtype)
    kk = float(topk)

    def tile_start(t):
        return pl.multiple_of(t * tk, tk)

    def key_pos(t):
        return tile_start(t) + lax.broadcasted_iota(jnp.int32, (tk, LANES), 0)

    def fold(x):
        return jnp.sum(x.reshape(tk // 8, 8, LANES), axis=0)

    def over_tiles(fn, init):
        return lax.fori_loop(0, n_tiles, fn, init)

    iq_heads = []
    for h in range(IDX_HEADS):
        t = iq_ref[:, (h // 2) * LANES:(h // 2 + 1) * LANES]
        iq_heads.append(jnp.where(even_head, t, zero) if h % 2 == 0 else jnp.where(even_head, zero, t))
    w = iw_ref[...]

    def index_scores(t, carry):
        k0 = tile_start(t)
        ikt = ik_ref[pl.ds(k0, tk), :]
        acc = jnp.zeros((tk, LANES), jnp.float32)
        for h in range(IDX_HEADS):
            logit = lax.dot_general(ikt, iq_heads[h], _NT, preferred_element_type=jnp.float32)
            acc = acc + jnp.maximum(logit, 0.0) * w[h:h + 1, :]
        sc_ref[pl.ds(k0, tk), :] = jnp.where(key_pos(t) < limit, acc, NEG_INF)
        return carry

    over_tiles(index_scores, 0)

    def count_ge(thr):
        def body(t, acc):
            x = sc_ref[pl.ds(tile_start(t), tk), :]
            return acc + fold(jnp.where(x >= thr, 1.0, 0.0))
        return jnp.sum(over_tiles(body, jnp.zeros((8, LANES), jnp.float32)), axis=0, keepdims=True)

    def write_mask(keep_fn):
        def body(t, carry):
            k0 = tile_start(t)
            x = sc_ref[pl.ds(k0, tk), :]
            kpos = key_pos(t)
            keep = jnp.logical_and(keep_fn(x, kpos), kpos < limit)
            sc_ref[pl.ds(k0, tk), :] = jnp.where(keep, 0.0, NEG_INF)
            return carry
        over_tiles(body, 0)

    @pl.when((i + 1) * Q_BLOCK <= topk)
    def _():
        write_mask(lambda x, kpos: kpos >= 0)

    @pl.when((i + 1) * Q_BLOCK > topk)
    def _():
        def min_max(t, c):
            x = sc_ref[pl.ds(tile_start(t), tk), :]
            adm = key_pos(t) < limit
            lo8 = jnp.minimum(c[0], jnp.min(jnp.where(adm, x, jnp.inf).reshape(tk // 8, 8, LANES), axis=0))
            hi8 = jnp.maximum(c[1], jnp.max(x.reshape(tk // 8, 8, LANES), axis=0))
            return lo8, hi8

        lo8, hi8 = over_tiles(min_max, (jnp.full((8, LANES), jnp.inf, jnp.float32),
                                        jnp.full((8, LANES), NEG_INF, jnp.float32)))
        lo = jnp.min(lo8, axis=0, keepdims=True)
        hi = jnp.max(hi8, axis=0, keepdims=True)

        def bisect(_, c):
            lo, hi = c
            mid = lo + (hi - lo) * 0.5
            ok = count_ge(mid) >= kk
            return jnp.where(ok, mid, lo), jnp.where(ok, hi, mid)

        lo, hi = lax.fori_loop(0, C_BISECT_STEPS, bisect, (lo, hi))

        def max_below(bound, strict):
            def body(t, acc):
                x = sc_ref[pl.ds(tile_start(t), tk), :]
                ok = (x < bound) if strict else (x <= bound)
                return jnp.maximum(acc, jnp.max(jnp.where(ok, x, NEG_INF).reshape(tk // 8, 8, LANES), axis=0))
            return jnp.max(over_tiles(body, jnp.full((8, LANES), NEG_INF, jnp.float32)), axis=0, keepdims=True)

        thr = max_below(hi, strict=False)
        cnt = count_ge(thr)

        def short(c):
            return jnp.max(jnp.where(c[1] < kk, 1, 0)) > 0

        def walk(c):
            thr, cnt = c
            nxt = max_below(thr, strict=True)
            ncnt = count_ge(nxt)
            move = cnt < kk
            return jnp.where(move, nxt, thr), jnp.where(move, ncnt, cnt)

        thr, cnt = lax.while_loop(short, walk, (thr, cnt))

        has_ties = jnp.max(jnp.where(cnt > kk, 1, 0)) > 0

        @pl.when(jnp.logical_not(has_ties))
        def _():
            write_mask(lambda x, kpos: x >= thr)

        @pl.when(has_ties)
        def _():
            def gt_body(t, acc):
                x = sc_ref[pl.ds(tile_start(t), tk), :]
                return acc + fold(jnp.where(x > thr, 1.0, 0.0))
            n_gt = jnp.sum(over_tiles(gt_body, jnp.zeros((8, LANES), jnp.float32)), axis=0, keepdims=True)
            need = kk - n_gt

            def ties_before(cut):
                def body(t, acc):
                    x = sc_ref[pl.ds(tile_start(t), tk), :]
                    hit = jnp.logical_and(x == thr, key_pos(t) < cut)
                    return acc + fold(jnp.where(hit, 1.0, 0.0))
                return jnp.sum(over_tiles(body, jnp.zeros((8, LANES), jnp.float32)), axis=0, keepdims=True)

            cut = jnp.zeros((1, LANES), jnp.int32)
            bit = seq
            while bit >= 1:
                cand = cut + bit
                cut = jnp.where(ties_before(cand) <= need, cand, cut)
                bit //= 2
            write_mask(lambda x, kpos: jnp.logical_or(x > thr, jnp.logical_and(x == thr, kpos < cut)))

    outs = []
    for h in range(C_HEADS):
        cols = slice((h // 2) * LANES, (h // 2 + 1) * LANES)
        qt = cq_ref[:, cols]
        qz = jnp.where(even_head, qt, zero) if h % 2 == 0 else jnp.where(even_head, zero, qt)

        def scores(t, m8, cols=cols, qz=qz):
            k0 = tile_start(t)
            kt = ck_ref[pl.ds(k0, tk), cols]
            s = lax.dot_general(kt, qz, _NT, preferred_element_type=jnp.float32) + sc_ref[pl.ds(k0, tk), :]
            s_ref[pl.ds(k0, tk), :] = s
            return jnp.maximum(m8, jnp.max(s.reshape(tk // 8, 8, LANES), axis=0))

        m = jnp.max(over_tiles(scores, jnp.full((8, LANES), NEG_INF, jnp.float32)), axis=0, keepdims=True)

        def weighted(t, c, h=h, m=m):
            l8, acc = c
            p = jnp.exp(s_ref[pl.ds(tile_start(t), tk), :] - m)
            vt = cvt_ref[t, h * HEAD_DIM:(h + 1) * HEAD_DIM, :]
            acc = acc + jnp.dot(vt, p.astype(_MXU_DTYPE), preferred_element_type=jnp.float32)
            return l8 + fold(p), acc

        l8, acc = over_tiles(weighted, (jnp.zeros((8, LANES), jnp.float32),
                                        jnp.zeros((HEAD_DIM, LANES), jnp.float32)))
        outs.append(acc * (1.0 / jnp.sum(l8, axis=0, keepdims=True)))
    o_ref[...] = jnp.concatenate(outs, axis=0).T.astype(o_ref.dtype)


def _dsa_attention(ik2, iq, iwt, ck, cvt, cq, bsz, seq, key_tile):
    m = iq.shape[0]
    nq = seq // Q_BLOCK
    topk = min(C_TOPK_MAX, seq // 4)
    assert topk % Q_BLOCK == 0
    return pl.pallas_call(
        functools.partial(_dsa_kernel, topk=topk, key_tile=key_tile, seq=seq),
        grid=(bsz, nq),
        in_specs=[
            pl.BlockSpec((seq, LANES), lambda b, i: (b, 0)),
            pl.BlockSpec((Q_BLOCK, IDX_HEADS * IDX_DIM), lambda b, i: (b * nq + i, 0)),
            pl.BlockSpec((None, IDX_HEADS, Q_BLOCK), lambda b, i: (b, 0, i)),
            pl.BlockSpec((seq, C_WIDTH), lambda b, i: (b, 0)),
            pl.BlockSpec((None, seq // key_tile, C_WIDTH, key_tile), lambda b, i: (b, 0, 0, 0)),
            pl.BlockSpec((Q_BLOCK, C_WIDTH), lambda b, i: (b * nq + i, 0)),
        ],
        out_specs=pl.BlockSpec((Q_BLOCK, C_WIDTH), lambda b, i: (b * nq + i, 0)),
        out_shape=jax.ShapeDtypeStruct((m, C_WIDTH), _MXU_DTYPE),
        scratch_shapes=[pltpu.VMEM((seq, LANES), jnp.float32), pltpu.VMEM((seq, LANES), jnp.float32)],
        compiler_params=_params(("parallel", "arbitrary")),
    )(ik2, iq, iwt, ck, cvt, cq)


def _layer_norm(z, g, b):
    mu = jnp.mean(z, axis=-1, keepdims=True)
    zc = z - mu
    var = jnp.mean(zc * zc, axis=-1, keepdims=True)
    return zc * lax.rsqrt(var + NORM_EPS) * g + b


def _post_kernel(x_ref, oa_ref, ob_ref, oc_ref, p_ref, wo_ref, wup_ref, wdown_ref, wgate_ref, wple_ref,
                 g1_ref, b1_ref, g2_ref, b2_ref, y_ref, *, alpha, ff_tile):
    y = jnp.dot(oa_ref[...], wo_ref[0:A_WIDTH, :], preferred_element_type=jnp.float32)
    y = y + jnp.dot(ob_ref[...], wo_ref[A_WIDTH:A_WIDTH + B_WIDTH, :], preferred_element_type=jnp.float32)
    y = y + jnp.dot(oc_ref[...], wo_ref[A_WIDTH + B_WIDTH:, :], preferred_element_type=jnp.float32)
    x1 = _layer_norm(alpha * x_ref[...] + y, g1_ref[...], b1_ref[...])
    x1b = x1.astype(_MXU_DTYPE)
    ff = jnp.zeros(x1.shape, jnp.float32)
    for f0 in range(0, wup_ref.shape[1], ff_tile):
        u = jnp.dot(x1b, wup_ref[:, f0:f0 + ff_tile], preferred_element_type=jnp.float32)
        u = jnp.square(jnp.maximum(u, 0.0)).astype(_MXU_DTYPE)
        ff = ff + jnp.dot(u, wdown_ref[f0:f0 + ff_tile, :], preferred_element_type=jnp.float32)
    gate = jax.nn.sigmoid(jnp.dot(x1b, wgate_ref[...], preferred_element_type=jnp.float32))
    ple = gate * jnp.dot(p_ref[...].astype(_MXU_DTYPE), wple_ref[...], preferred_element_type=jnp.float32)
    y_ref[...] = _layer_norm(alpha * x1 + ff + ple, g2_ref[...], b2_ref[...])


def _post(x, oa, ob, oc, p, wo, wup, wdown, wgate, wple, g1, b1, g2, b2, alpha, seq):
    m, d = x.shape
    tm = min(seq, 512)

    def rows(width):
        return pl.BlockSpec((tm, width), lambda i: (i, 0))

    weights = [wo, wup, wdown, wgate, wple, g1, b1, g2, b2]
    return pl.pallas_call(
        functools.partial(_post_kernel, alpha=alpha, ff_tile=512),
        grid=(m // tm,),
        in_specs=[rows(d), rows(A_WIDTH), rows(B_WIDTH), rows(C_WIDTH), rows(p.shape[1])]
                 + [_resident(t.shape) for t in weights],
        out_specs=rows(d),
        out_shape=jax.ShapeDtypeStruct((m, d), jnp.float32),
        compiler_params=_params(("parallel",)),
    )(x, oa, ob, oc, p, *weights)


def kernel(x, p, positions, w_in, rel_bias, lam_q1, lam_k1, lam_q2, lam_k2, diff_norm_g, w_o,
           ln1_g, ln1_b, w_up, w_down, w_ple_gate, w_ple, ln2_g, ln2_b):
    bsz, seq, d_model = x.shape
    depth = w_in.shape[0]
    m = bsz * seq
    assert seq % Q_BLOCK == 0 and w_in.shape[2] == IN_TOTAL
    key_tile = min(seq, 256)
    alpha = (2 * depth) ** 0.25
    cast = lambda t: t.astype(_MXU_DTYPE)

    cos, sin = _rope_tables(positions)
    xf = x.reshape(m, d_model)
    for i in range(depth):
        w = jnp.pad(cast(w_in[i]), ((0, 0), (0, IN_PADDED - IN_TOTAL)))
        a_qkv, bq, bk, bv, cq, ck, cvt, iq, ik2, iwt = _in_proj(xf, w, cos, sin, bsz, seq, key_tile)
        lam_init = 0.8 - 0.6 * math.exp(-0.3 * i)
        lam_vecs = jnp.stack([lam_q1[i], lam_k1[i], lam_q2[i], lam_k2[i]]).astype(jnp.float32)
        o_a = _band_attention(a_qkv, _band_bias(rel_bias[i]), bsz, seq)
        o_b = _diff_attention(bq, bk, bv, lam_vecs, diff_norm_g[i][None, :].astype(jnp.float32),
                              lam_init, bsz, seq)
        o_c = _dsa_attention(ik2, iq, iwt, ck, cvt, cq, bsz, seq, key_tile)
        row = lambda t: t[i][None, :].astype(jnp.float32)
        xf = _post(xf, o_a, o_b, o_c, p[i].reshape(m, p.shape[-1]),
                   cast(w_o[i]), cast(w_up[i]), cast(w_down[i]), cast(w_ple_gate[i]), cast(w_ple[i]),
                   row(ln1_g), row(ln1_b), row(ln2_g), row(ln2_b), alpha, seq)
    return xf.reshape(bsz, seq, d_model)
```

```python
import functools
import math

import jax
import jax.numpy as jnp
import numpy as np
from jax import lax
from jax.experimental import pallas as pl
from jax.experimental.pallas import tpu as pltpu

CHUNK = 64
HEAD_DIM = 64
A_HEADS = 4
A_LEFT_CHUNKS = 8
REL_CLIP = 128
B_HEADS = 4
B_V_DIM = 2 * HEAD_DIM
C_HEADS = 4
C_TOPK_MAX = 256
IDX_HEADS = 8
IDX_DIM = 64
A_WIDTH = A_HEADS * HEAD_DIM
B_WIDTH = B_HEADS * B_V_DIM
C_WIDTH = C_HEADS * HEAD_DIM
ROPE_THETA = 10000.0
NORM_EPS = 1e-5
NEG_INF = -1e30

LANES = 128
SUBLANES = 8
Q_BLOCK = 128
C_Q_BLOCK = 256
VMEM_LIMIT = 56 * 1024 * 1024

OFF_A = 0
OFF_BQ = 3 * A_WIDTH
OFF_BK = OFF_BQ + 2 * B_HEADS * HEAD_DIM
OFF_BV = OFF_BK + 2 * B_HEADS * HEAD_DIM
OFF_CQ = OFF_BV + B_WIDTH
OFF_CK = OFF_CQ + C_WIDTH
OFF_CV = OFF_CK + C_WIDTH
OFF_IQ = OFF_CV + C_WIDTH
OFF_IK = OFF_IQ + IDX_HEADS * IDX_DIM
IN_TOTAL = OFF_IK + IDX_DIM + IDX_HEADS
IN_PADDED = OFF_IK + LANES

_MXU_DTYPE = jnp.bfloat16
_NT = (((1,), (1,)), ((), ()))


def _params(sem):
    return pltpu.CompilerParams(dimension_semantics=sem, vmem_limit_bytes=VMEM_LIMIT)


def _resident(shape):
    return pl.BlockSpec(shape, lambda *_: (0,) * len(shape), pipeline_mode=pl.Buffered(1))


def _split_heads(t, even):
    zero = jnp.zeros((), t.dtype)
    return jnp.where(even, t, zero), jnp.where(even, zero, t)


def _rope_table_kernel(pos_ref, inv_ref, sgn_ref, cos_ref, sin_ref):
    ang = pos_ref[...].astype(jnp.float32) * inv_ref[...]
    cos_ref[...] = jnp.cos(ang)
    sin_ref[...] = jnp.sin(ang) * sgn_ref[...]


def _rope_tables(positions):
    m = positions.size
    tm = min(m, 2048)
    inv = ROPE_THETA ** (-jnp.arange(0, HEAD_DIM, 2, dtype=jnp.float32) / HEAD_DIM)
    inv = jnp.tile(inv, LANES // (HEAD_DIM // 2))[None, :]
    sgn = jnp.tile(jnp.concatenate([-jnp.ones(HEAD_DIM // 2), jnp.ones(HEAD_DIM // 2)]),
                   LANES // HEAD_DIM).astype(jnp.float32)[None, :]
    row = pl.BlockSpec((tm, LANES), lambda i: (i, 0))
    const = pl.BlockSpec((1, LANES), lambda i: (0, 0))
    return pl.pallas_call(
        _rope_table_kernel,
        grid=(m // tm,),
        in_specs=[pl.BlockSpec((tm, 1), lambda i: (i, 0)), const, const],
        out_specs=[row, row],
        out_shape=[jax.ShapeDtypeStruct((m, LANES), jnp.float32)] * 2,
        compiler_params=_params(("parallel",)),
        name="rope_tables",
    )(positions.reshape(m, 1), inv, sgn)


def _in_proj_kernel(x_ref, w_ref, cos_ref, sin_ref,
                    a_ref, bq_ref, bk_ref, bvt_ref, cq_ref, ck_ref, cvt_ref,
                    iq_ref, ik2_ref, iwt_ref, *, key_tile):
    tm = x_ref.shape[0]
    xb = x_ref[...].astype(_MXU_DTYPE)
    cos = cos_ref[...]
    sin = sin_ref[...]
    lane = lax.broadcasted_iota(jnp.int32, (tm, LANES), 1)
    low_half = (lane & (HEAD_DIM - 1)) < HEAD_DIM // 2
    qk_scale = HEAD_DIM ** -0.5

    def proj(c0, n):
        return jnp.dot(xb, w_ref[:, c0:c0 + n], preferred_element_type=jnp.float32)

    def rope(t):
        rot = jnp.where(low_half, pltpu.roll(t, LANES - HEAD_DIM // 2, 1),
                        pltpu.roll(t, HEAD_DIM // 2, 1))
        return t * cos + rot * sin

    def store(ref, h, roped=False, scale=None, col0=0):
        for j in range(h.shape[1] // LANES):
            t = h[:, j * LANES:(j + 1) * LANES]
            if roped:
                t = rope(t)
            if scale is not None:
                t = t * scale
            ref[:, col0 + j * LANES:col0 + (j + 1) * LANES] = t.astype(ref.dtype)

    store(a_ref, proj(OFF_A, A_WIDTH), scale=qk_scale)
    store(a_ref, proj(OFF_A + A_WIDTH, 2 * A_WIDTH), col0=A_WIDTH)
    store(bq_ref, proj(OFF_BQ, OFF_BK - OFF_BQ), roped=True, scale=qk_scale)
    store(bk_ref, proj(OFF_BK, OFF_BV - OFF_BK), roped=True)
    bvt_ref[0] = proj(OFF_BV, B_WIDTH).T.astype(bvt_ref.dtype)
    store(cq_ref, proj(OFF_CQ, C_WIDTH), roped=True, scale=qk_scale)
    store(ck_ref, proj(OFF_CK, C_WIDTH), roped=True)
    cv = proj(OFF_CV, C_WIDTH)
    for c in range(tm // key_tile):
        cvt_ref[c] = cv[c * key_tile:(c + 1) * key_tile, :].T.astype(cvt_ref.dtype)
    store(iq_ref, proj(OFF_IQ, IDX_HEADS * IDX_DIM), roped=True)
    last = proj(OFF_IK, LANES)
    ikr = rope(last)
    ik2 = jnp.where(lane < IDX_DIM, ikr, pltpu.roll(ikr, IDX_DIM, 1))
    ik2_ref[...] = ik2.astype(ik2_ref.dtype)
    iwt_ref[...] = last.T[IDX_DIM:IDX_DIM + IDX_HEADS, :] * ((IDX_HEADS * IDX_DIM) ** -0.5)


def _in_proj(x, w, cos, sin, bsz, seq, key_tile):
    m, d = x.shape
    tm = min(seq, 512)
    n_s = seq // tm
    f = _MXU_DTYPE

    def rows(width):
        return pl.BlockSpec((tm, width), lambda i: (i, 0))

    out_shape = [
        jax.ShapeDtypeStruct((m, 3 * A_WIDTH), f),
        jax.ShapeDtypeStruct((m, OFF_BK - OFF_BQ), f),
        jax.ShapeDtypeStruct((m, OFF_BV - OFF_BK), f),
        jax.ShapeDtypeStruct((bsz, n_s, B_WIDTH, tm), f),
        jax.ShapeDtypeStruct((m, C_WIDTH), f),
        jax.ShapeDtypeStruct((m, C_WIDTH), f),
        jax.ShapeDtypeStruct((bsz, seq // key_tile, C_WIDTH, key_tile), f),
        jax.ShapeDtypeStruct((m, IDX_HEADS * IDX_DIM), f),
        jax.ShapeDtypeStruct((m, LANES), f),
        jax.ShapeDtypeStruct((bsz, IDX_HEADS, seq), jnp.float32),
    ]
    out_specs = [
        rows(3 * A_WIDTH), rows(OFF_BK - OFF_BQ), rows(OFF_BV - OFF_BK),
        pl.BlockSpec((None, 1, B_WIDTH, tm), lambda i: (i // n_s, i % n_s, 0, 0)),
        rows(C_WIDTH), rows(C_WIDTH),
        pl.BlockSpec((None, tm // key_tile, C_WIDTH, key_tile), lambda i: (i // n_s, i % n_s, 0, 0)),
        rows(IDX_HEADS * IDX_DIM), rows(LANES),
        pl.BlockSpec((None, IDX_HEADS, tm), lambda i: (i // n_s, 0, i % n_s)),
    ]
    return pl.pallas_call(
        functools.partial(_in_proj_kernel, key_tile=key_tile),
        grid=(m // tm,),
        in_specs=[rows(d), _resident(w.shape), rows(LANES), rows(LANES)],
        out_specs=out_specs,
        out_shape=out_shape,
        compiler_params=_params(("parallel",)),
        name="in_proj",
    )(x, w, cos, sin)


A_WINDOW_TILES = (A_LEFT_CHUNKS * CHUNK) // Q_BLOCK + 1


def _band_bias(rel_bias):
    width = A_WINDOW_TILES * Q_BLOCK
    span = Q_BLOCK + width - 1
    d = np.arange(span) - (width - 1) + A_LEFT_CHUNKS * CHUNK
    gen = rel_bias[:, np.clip(d, -REL_CLIP, REL_CLIP) + REL_CLIP].astype(jnp.float32)
    sheared = jnp.tile(gen, (1, Q_BLOCK + 1))[:, :Q_BLOCK * (span + 1)].reshape(-1, Q_BLOCK, span + 1)
    bias = sheared[:, :, :width][:, :, ::-1]
    r = np.arange(Q_BLOCK)[:, None]
    e = np.arange(width)[None, :]
    back = r // CHUNK + A_LEFT_CHUNKS - e // CHUNK
    in_band = (back >= 0) & (back <= A_LEFT_CHUNKS)
    return jnp.where(jnp.asarray(in_band)[None], bias, NEG_INF)


def _band_kernel(q_ref, k_ref, v_ref, bias_ref, o_ref):
    i = pl.program_id(1)
    lane = lax.broadcasted_iota(jnp.int32, (Q_BLOCK, LANES), 1)
    even_head = lane < HEAD_DIM
    for pair in range(A_HEADS // 2):
        cols = slice(pair * LANES, (pair + 1) * LANES)
        qs = jnp.concatenate(_split_heads(q_ref[:, cols], even_head), axis=0)
        s_tiles = []
        for tt in range(A_WINDOW_TILES):
            t = i - (A_WINDOW_TILES - 1) + tt
            k0 = pl.multiple_of(jnp.maximum(t, 0) * Q_BLOCK, Q_BLOCK)
            kt = k_ref[pl.ds(k0, Q_BLOCK), cols]
            s = lax.dot_general(qs, kt, _NT, preferred_element_type=jnp.float32)
            lanes_tt = slice(tt * Q_BLOCK, (tt + 1) * Q_BLOCK)
            b = jnp.concatenate([bias_ref[2 * pair, :, lanes_tt], bias_ref[2 * pair + 1, :, lanes_tt]], axis=0)
            s_tiles.append(jnp.where(t >= 0, s + b, NEG_INF))
        m = functools.reduce(jnp.maximum, [jnp.max(s, axis=1, keepdims=True) for s in s_tiles])
        p_tiles = [jnp.exp(s - m) for s in s_tiles]
        l = functools.reduce(jnp.add, [jnp.sum(p, axis=1, keepdims=True) for p in p_tiles])
        acc = jnp.zeros((2 * Q_BLOCK, LANES), jnp.float32)
        for tt in range(A_WINDOW_TILES):
            t = i - (A_WINDOW_TILES - 1) + tt
            k0 = pl.multiple_of(jnp.maximum(t, 0) * Q_BLOCK, Q_BLOCK)
            vt = v_ref[pl.ds(k0, Q_BLOCK), cols]
            acc = acc + jnp.dot(p_tiles[tt].astype(_MXU_DTYPE), vt, preferred_element_type=jnp.float32)
        acc = acc * (1.0 / l)
        o_ref[:, cols] = jnp.where(even_head, acc[:Q_BLOCK], acc[Q_BLOCK:]).astype(o_ref.dtype)


def _band_attention(a_qkv, bias, bsz, seq):
    m = a_qkv.shape[0]
    nq = seq // Q_BLOCK
    return pl.pallas_call(
        _band_kernel,
        grid=(bsz, nq),
        in_specs=[
            pl.BlockSpec((Q_BLOCK, A_WIDTH), lambda b, i: (b * nq + i, 0)),
            pl.BlockSpec((seq, A_WIDTH), lambda b, i: (b, 1)),
            pl.BlockSpec((seq, A_WIDTH), lambda b, i: (b, 2)),
            _resident(bias.shape),
        ],
        out_specs=pl.BlockSpec((Q_BLOCK, A_WIDTH), lambda b, i: (b * nq + i, 0)),
        out_shape=jax.ShapeDtypeStruct((m, A_WIDTH), _MXU_DTYPE),
        compiler_params=_params(("parallel", "arbitrary")),
        name="band_attention",
    )(a_qkv, a_qkv, a_qkv, bias)


def _diff_kernel(lam_ref, g_ref, q_ref, k_ref, vt_ref, o_ref, qs_ref, acc_ref, *, lam_init, key_tile):
    i = pl.program_id(1)
    tk = key_tile
    nq2 = 2 * Q_BLOCK
    lv = lam_ref[...]
    lam = (jnp.exp(jnp.sum(lv[0:1] * lv[1:2], axis=1, keepdims=True))
           - jnp.exp(jnp.sum(lv[2:3] * lv[3:4], axis=1, keepdims=True)) + lam_init)
    first_map = lax.broadcasted_iota(jnp.int32, (Q_BLOCK, LANES), 1) < HEAD_DIM
    for h in range(B_HEADS):
        qs_ref[h] = jnp.concatenate(_split_heads(q_ref[:, h * LANES:(h + 1) * LANES], first_map), axis=0)
    acc_ref[...] = jnp.zeros(acc_ref.shape, jnp.float32)
    col = lax.broadcasted_iota(jnp.int32, (1, nq2), 1)
    limit = jnp.where((col & (Q_BLOCK - 1)) < CHUNK, i * Q_BLOCK + CHUNK, (i + 1) * Q_BLOCK)
    n_tiles = ((i + 1) * Q_BLOCK + tk - 1) // tk

    def fold(x, op):
        return op(x.reshape(tk // SUBLANES, SUBLANES, nq2), axis=0)

    def tile(t, stats, masked):
        k0 = pl.multiple_of(t * tk, tk)
        scores = [lax.dot_general(k_ref[pl.ds(k0, tk), h * LANES:(h + 1) * LANES], qs_ref[h], _NT,
                                  preferred_element_type=jnp.float32) for h in range(B_HEADS)]
        new_stats, weights = [], []
        for h in range(B_HEADS):
            s = scores[h]
            if masked:
                s = jnp.where(k0 + lax.broadcasted_iota(jnp.int32, s.shape, 0) < limit, s, NEG_INF)
            m_old, l_old = stats[h]
            m_new = jnp.maximum(m_old, jnp.max(fold(s, jnp.max), axis=0, keepdims=True))
            alpha = jnp.exp(m_old - m_new)
            p = jnp.exp(s - m_new)
            new_stats.append((m_new, alpha * l_old + jnp.sum(fold(p, jnp.sum), axis=0, keepdims=True)))
            weights.append((alpha, p.astype(_MXU_DTYPE)))
        for h in range(B_HEADS):
            alpha, p = weights[h]
            acc_ref[h] = alpha * acc_ref[h] + jnp.dot(vt_ref[t, h * B_V_DIM:(h + 1) * B_V_DIM, :], p,
                                                      preferred_element_type=jnp.float32)
        return tuple(new_stats)

    stats = tuple((jnp.full((1, nq2), NEG_INF, jnp.float32), jnp.zeros((1, nq2), jnp.float32))
                  for _ in range(B_HEADS))
    stats = lax.fori_loop(0, n_tiles - 1, functools.partial(tile, masked=False), stats)
    stats = tile(n_tiles - 1, stats, masked=True)

    for h in range(B_HEADS):
        acc = acc_ref[h] * (1.0 / stats[h][1])
        o = (acc[:, :Q_BLOCK] - acc[:, Q_BLOCK:] * lam).T
        ms = jnp.mean(o * o, axis=1, keepdims=True)
        o = o * lax.rsqrt(ms + NORM_EPS) * g_ref[...] * (1.0 - lam_init)
        o_ref[:, h * B_V_DIM:(h + 1) * B_V_DIM] = o.astype(o_ref.dtype)


def _diff_attention(bq, bk, bvt, lam_vecs, gain, lam_init, bsz, seq):
    m = bq.shape[0]
    nq = seq // Q_BLOCK
    n_kt, key_tile = bvt.shape[1], bvt.shape[3]
    return pl.pallas_call(
        functools.partial(_diff_kernel, lam_init=lam_init, key_tile=key_tile),
        grid=(bsz, nq),
        in_specs=[
            _resident(lam_vecs.shape),
            _resident(gain.shape),
            pl.BlockSpec((Q_BLOCK, B_HEADS * LANES), lambda b, i: (b * nq + i, 0)),
            pl.BlockSpec((seq, B_HEADS * LANES), lambda b, i: (b, 0)),
            pl.BlockSpec((None, n_kt, B_WIDTH, key_tile), lambda b, i: (b, 0, 0, 0)),
        ],
        out_specs=pl.BlockSpec((Q_BLOCK, B_WIDTH), lambda b, i: (b * nq + i, 0)),
        out_shape=jax.ShapeDtypeStruct((m, B_WIDTH), _MXU_DTYPE),
        scratch_shapes=[
            pltpu.VMEM((B_HEADS, 2 * Q_BLOCK, LANES), _MXU_DTYPE),
            pltpu.VMEM((B_HEADS, B_V_DIM, 2 * Q_BLOCK), jnp.float32),
        ],
        compiler_params=_params(("parallel", "arbitrary")),
        name="diff_attention",
    )(lam_vecs, gain, bq, bk, bvt)


C_BISECT_STEPS = 14


def _dsa_kernel(ik_ref, iq_ref, iw_ref, ck_ref, cvt_ref, cq_ref, o_ref,
                sc_ref, iqz_ref, qz_ref, acc_ref, *, topk, key_tile, sel_tile, seq):
    i = pl.program_id(1)
    tk = key_tile
    cq = C_Q_BLOCK
    n_keys = (i + 1) * cq
    n_tiles = (n_keys + tk - 1) // tk
    n_sel = (n_keys + sel_tile - 1) // sel_tile
    lane_q = lax.broadcasted_iota(jnp.int32, (1, cq), 1)
    limit = i * cq + (lane_q // CHUNK + 1) * CHUNK
    searched = limit > topk
    even_head = lax.broadcasted_iota(jnp.int32, (cq, LANES), 1) < HEAD_DIM
    kk = float(topk)

    def fold(x, op=jnp.sum):
        return op(x.reshape(x.shape[0] // SUBLANES, SUBLANES, cq), axis=0)

    def key_pos(k0, rows):
        return k0 + lax.broadcasted_iota(jnp.int32, (rows, cq), 0)

    for j in range(IDX_HEADS // 2):
        iqz_ref[2 * j], iqz_ref[2 * j + 1] = _split_heads(iq_ref[:, j * LANES:(j + 1) * LANES], even_head)
    for j in range(C_HEADS // 2):
        qz_ref[2 * j], qz_ref[2 * j + 1] = _split_heads(cq_ref[:, j * LANES:(j + 1) * LANES], even_head)
    w = iw_ref[...]

    def index_scores(t, c):
        k0 = pl.multiple_of(t * tk, tk)
        ikt = ik_ref[pl.ds(k0, tk), :]
        acc = jnp.zeros((tk, cq), jnp.float32)
        for h in range(IDX_HEADS):
            logit = lax.dot_general(ikt, iqz_ref[h], _NT, preferred_element_type=jnp.float32)
            acc = acc + jnp.maximum(logit, 0.0) * w[h:h + 1, :]
        adm = key_pos(k0, tk) < limit
        sc_ref[pl.ds(k0, tk), :] = jnp.where(adm, acc, NEG_INF)
        return (jnp.minimum(c[0], fold(jnp.where(adm, acc, jnp.inf), jnp.min)),
                jnp.maximum(c[1], fold(jnp.where(adm, acc, NEG_INF), jnp.max)))

    lo8, hi8 = lax.fori_loop(0, n_tiles, index_scores,
                             (jnp.full((SUBLANES, cq), jnp.inf, jnp.float32),
                              jnp.full((SUBLANES, cq), NEG_INF, jnp.float32)))

    @pl.when(n_tiles * tk < n_sel * sel_tile)
    def _():
        k0 = pl.multiple_of(n_tiles * tk, tk)
        sc_ref[pl.ds(k0, tk), :] = jnp.full((tk, cq), NEG_INF, jnp.float32)

    def over_scores(fn, init):
        def body(t, acc):
            k0 = pl.multiple_of(t * sel_tile, sel_tile)
            return fn(sc_ref[pl.ds(k0, sel_tile), :], k0, acc)
        return lax.fori_loop(0, n_sel, body, init)

    def count(pred):
        part = over_scores(lambda x, k0, acc: acc + fold(jnp.where(pred(x, k0), 1.0, 0.0)),
                           jnp.zeros((SUBLANES, cq), jnp.float32))
        return jnp.sum(part, axis=0, keepdims=True)

    def count_ge(thr):
        return count(lambda x, k0: x >= thr)

    def max_where(pred):
        part = over_scores(lambda x, k0, acc: jnp.maximum(acc, fold(jnp.where(pred(x), x, NEG_INF), jnp.max)),
                           jnp.full((SUBLANES, cq), NEG_INF, jnp.float32))
        return jnp.max(part, axis=0, keepdims=True)

    def write_mask(keep_fn):
        def body(t, carry):
            k0 = pl.multiple_of(t * tk, tk)
            x = sc_ref[pl.ds(k0, tk), :]
            kpos = key_pos(k0, tk)
            keep = jnp.logical_and(keep_fn(x, kpos), kpos < limit)
            sc_ref[pl.ds(k0, tk), :] = jnp.where(keep, 0.0, NEG_INF)
            return carry
        lax.fori_loop(0, n_tiles, body, 0)

    @pl.when(n_keys <= topk)
    def _():
        write_mask(lambda x, kpos: kpos >= 0)

    @pl.when(n_keys > topk)
    def _():
        lo = jnp.min(lo8, axis=0, keepdims=True)
        hi = jnp.max(hi8, axis=0, keepdims=True)

        def bisect(_, c):
            lo, hi = c
            mid = lo + (hi - lo) * 0.5
            ok = count_ge(mid) >= kk
            return jnp.where(ok, mid, lo), jnp.where(ok, hi, mid)

        lo, hi = lax.fori_loop(0, C_BISECT_STEPS, bisect, (lo, hi))

        def settled(cnt):
            return jnp.logical_or(cnt >= kk, jnp.logical_not(searched))

        thr = max_where(lambda x: x <= hi)
        cnt = count_ge(thr)

        def unsettled(c):
            return jnp.max(jnp.where(settled(c[1]), 0, 1)) > 0

        def walk(c):
            thr, cnt = c
            nxt = max_where(lambda x: x < thr)
            ncnt = count_ge(nxt)
            stay = settled(cnt)
            return jnp.where(stay, thr, nxt), jnp.where(stay, cnt, ncnt)

        thr, cnt = lax.while_loop(unsettled, walk, (thr, cnt))
        thr = jnp.where(searched, thr, NEG_INF)
        has_ties = jnp.max(jnp.where(jnp.logical_and(searched, cnt > kk), 1, 0)) > 0

        @pl.when(jnp.logical_not(has_ties))
        def _():
            write_mask(lambda x, kpos: x >= thr)

        @pl.when(has_ties)
        def _():
            need = kk - count(lambda x, k0: x > thr)
            cut = jnp.zeros((1, cq), jnp.int32)
            bit = seq
            while bit >= 1:
                cand = cut + bit
                before = count(lambda x, k0: jnp.logical_and(x == thr, key_pos(k0, sel_tile) < cand))
                cut = jnp.where(before <= need, cand, cut)
                bit //= 2
            cut = jnp.where(searched, cut, 2 * seq)
            write_mask(lambda x, kpos: jnp.logical_or(x > thr, jnp.logical_and(x == thr, kpos < cut)))

    acc_ref[...] = jnp.zeros(acc_ref.shape, jnp.float32)

    def attend(t, c):
        k0 = pl.multiple_of(t * tk, tk)
        mask = sc_ref[pl.ds(k0, tk), :]
        scores = [lax.dot_general(ck_ref[pl.ds(k0, tk), (h // 2) * LANES:(h // 2 + 1) * LANES], qz_ref[h], _NT,
                                  preferred_element_type=jnp.float32) for h in range(C_HEADS)]
        out, weights = [], []
        for h in range(C_HEADS):
            m_old, l_old = c[h]
            s = scores[h] + mask
            m_new = jnp.maximum(m_old, jnp.max(fold(s, jnp.max), axis=0, keepdims=True))
            alpha = jnp.exp(m_old - m_new)
            p = jnp.exp(s - m_new)
            out.append((m_new, alpha * l_old + jnp.sum(fold(p), axis=0, keepdims=True)))
            weights.append((alpha, p.astype(_MXU_DTYPE)))
        for h in range(C_HEADS):
            alpha, p = weights[h]
            acc_ref[h] = alpha * acc_ref[h] + jnp.dot(cvt_ref[t, h * HEAD_DIM:(h + 1) * HEAD_DIM, :], p,
                                                      preferred_element_type=jnp.float32)
        return tuple(out)

    init = tuple((jnp.full((1, cq), NEG_INF, jnp.float32), jnp.zeros((1, cq), jnp.float32))
                 for _ in range(C_HEADS))
    stats = lax.fori_loop(0, n_tiles, attend, init)
    out = jnp.concatenate([acc_ref[h] * (1.0 / stats[h][1]) for h in range(C_HEADS)], axis=0)
    o_ref[...] = out.T.astype(o_ref.dtype)


def _dsa_attention(ik2, iq, iwt, ck, cvt, cq, bsz, seq, key_tile):
    m = iq.shape[0]
    cqb = C_Q_BLOCK
    nq = seq // cqb
    topk = min(C_TOPK_MAX, seq // 4)
    sel_tile = min(seq, 512)
    assert key_tile == cqb and seq % sel_tile == 0 and sel_tile % key_tile == 0
    return pl.pallas_call(
        functools.partial(_dsa_kernel, topk=topk, key_tile=key_tile, sel_tile=sel_tile, seq=seq),
        grid=(bsz, nq),
        in_specs=[
            pl.BlockSpec((seq, LANES), lambda b, i: (b, 0)),
            pl.BlockSpec((cqb, IDX_HEADS * IDX_DIM), lambda b, i: (b * nq + i, 0)),
            pl.BlockSpec((None, IDX_HEADS, cqb), lambda b, i: (b, 0, i)),
            pl.BlockSpec((seq, C_WIDTH), lambda b, i: (b, 0)),
            pl.BlockSpec((None, seq // key_tile, C_WIDTH, key_tile), lambda b, i: (b, 0, 0, 0)),
            pl.BlockSpec((cqb, C_WIDTH), lambda b, i: (b * nq + i, 0)),
        ],
        out_specs=pl.BlockSpec((cqb, C_WIDTH), lambda b, i: (b * nq + i, 0)),
        out_shape=jax.ShapeDtypeStruct((m, C_WIDTH), _MXU_DTYPE),
        scratch_shapes=[
            pltpu.VMEM((seq, cqb), jnp.float32),
            pltpu.VMEM((IDX_HEADS, cqb, LANES), _MXU_DTYPE),
            pltpu.VMEM((C_HEADS, cqb, LANES), _MXU_DTYPE),
            pltpu.VMEM((C_HEADS, HEAD_DIM, cqb), jnp.float32),
        ],
        compiler_params=_params(("parallel", "arbitrary")),
        name="dsa_attention",
    )(ik2, iq, iwt, ck, cvt, cq)


def _layer_norm(z, g, b):
    mu = jnp.mean(z, axis=-1, keepdims=True)
    zc = z - mu
    var = jnp.mean(zc * zc, axis=-1, keepdims=True)
    return zc * lax.rsqrt(var + NORM_EPS) * g + b


def _post_kernel(x_ref, oa_ref, ob_ref, oc_ref, p_ref, wo_ref, wup_ref, wdown_ref, wgate_ref, wple_ref,
                 g1_ref, b1_ref, g2_ref, b2_ref, y_ref, *, alpha, ff_tile):
    y = jnp.dot(oa_ref[...], wo_ref[0:A_WIDTH, :], preferred_element_type=jnp.float32)
    y = y + jnp.dot(ob_ref[...], wo_ref[A_WIDTH:A_WIDTH + B_WIDTH, :], preferred_element_type=jnp.float32)
    y = y + jnp.dot(oc_ref[...], wo_ref[A_WIDTH + B_WIDTH:, :], preferred_element_type=jnp.float32)
    x1 = _layer_norm(alpha * x_ref[...] + y, g1_ref[...], b1_ref[...])
    x1b = x1.astype(_MXU_DTYPE)
    ff = jnp.zeros(x1.shape, jnp.float32)
    for f0 in range(0, wup_ref.shape[1], ff_tile):
        u = jnp.dot(x1b, wup_ref[:, f0:f0 + ff_tile], preferred_element_type=jnp.float32)
        u = jnp.square(jnp.maximum(u, 0.0)).astype(_MXU_DTYPE)
        ff = ff + jnp.dot(u, wdown_ref[f0:f0 + ff_tile, :], preferred_element_type=jnp.float32)
    gate = jax.nn.sigmoid(jnp.dot(x1b, wgate_ref[...], preferred_element_type=jnp.float32))
    ple = gate * jnp.dot(p_ref[...].astype(_MXU_DTYPE), wple_ref[...], preferred_element_type=jnp.float32)
    y_ref[...] = _layer_norm(alpha * x1 + ff + ple, g2_ref[...], b2_ref[...])


def _post(x, oa, ob, oc, p, wo, wup, wdown, wgate, wple, g1, b1, g2, b2, alpha, seq):
    m, d = x.shape
    tm = min(seq, 512)

    def rows(width):
        return pl.BlockSpec((tm, width), lambda i: (i, 0))

    weights = [wo, wup, wdown, wgate, wple, g1, b1, g2, b2]
    return pl.pallas_call(
        functools.partial(_post_kernel, alpha=alpha, ff_tile=512),
        grid=(m // tm,),
        in_specs=[rows(d), rows(A_WIDTH), rows(B_WIDTH), rows(C_WIDTH), rows(p.shape[1])]
                 + [_resident(t.shape) for t in weights],
        out_specs=rows(d),
        out_shape=jax.ShapeDtypeStruct((m, d), jnp.float32),
        compiler_params=_params(("parallel",)),
        name="post_mlp",
    )(x, oa, ob, oc, p, *weights)


def kernel(x, p, positions, w_in, rel_bias, lam_q1, lam_k1, lam_q2, lam_k2, diff_norm_g, w_o,
           ln1_g, ln1_b, w_up, w_down, w_ple_gate, w_ple, ln2_g, ln2_b):
    bsz, seq, d_model = x.shape
    depth = w_in.shape[0]
    m = bsz * seq
    assert seq % C_Q_BLOCK == 0 and w_in.shape[2] == IN_TOTAL
    key_tile = C_Q_BLOCK
    alpha = (2 * depth) ** 0.25
    cast = lambda t: t.astype(_MXU_DTYPE)

    cos, sin = _rope_tables(positions)
    xf = x.reshape(m, d_model)
    for i in range(depth):
        w = jnp.pad(cast(w_in[i]), ((0, 0), (0, IN_PADDED - IN_TOTAL)))
        a_qkv, bq, bk, bvt, cq, ck, cvt, iq, ik2, iwt = _in_proj(xf, w, cos, sin, bsz, seq, key_tile)
        lam_init = 0.8 - 0.6 * math.exp(-0.3 * i)
        lam_vecs = jnp.stack([lam_q1[i], lam_k1[i], lam_q2[i], lam_k2[i]]).astype(jnp.float32)
        o_a = _band_attention(a_qkv, _band_bias(rel_bias[i]), bsz, seq)
        o_b = _diff_attention(bq, bk, bvt, lam_vecs, diff_norm_g[i][None, :].astype(jnp.float32),
                              lam_init, bsz, seq)
        o_c = _dsa_attention(ik2, iq, iwt, ck, cvt, cq, bsz, seq, key_tile)
        row = lambda t: t[i][None, :].astype(jnp.float32)
        xf = _post(xf, o_a, o_b, o_c, p[i].reshape(m, p.shape[-1]),
                   cast(w_o[i]), cast(w_up[i]), cast(w_down[i]), cast(w_ple_gate[i]), cast(w_ple[i]),
                   row(ln1_g), row(ln1_b), row(ln2_g), row(ln2_b), alpha, seq)
    return xf.reshape(bsz, seq, d_model)
```

```python
import functools
import math

import jax
import jax.numpy as jnp
import numpy as np
from jax import lax
from jax.experimental import pallas as pl
from jax.experimental.pallas import tpu as pltpu

CHUNK = 64
HEAD_DIM = 64
A_HEADS = 4
A_LEFT_CHUNKS = 8
REL_CLIP = 128
B_HEADS = 4
B_V_DIM = 2 * HEAD_DIM
C_HEADS = 4
C_TOPK_MAX = 256
IDX_HEADS = 8
IDX_DIM = 64
A_WIDTH = A_HEADS * HEAD_DIM
B_WIDTH = B_HEADS * B_V_DIM
C_WIDTH = C_HEADS * HEAD_DIM
ROPE_THETA = 10000.0
NORM_EPS = 1e-5
NEG_INF = -1e30
LOG2_E = math.log2(math.e)

LANES = 128
FOLD_ROWS = 64
Q_BLOCK = 128
B_Q_BLOCK = 256
C_Q_BLOCK = 256
VMEM_LIMIT = 56 * 1024 * 1024

OFF_A = 0
OFF_BQ = 3 * A_WIDTH
OFF_BK = OFF_BQ + 2 * B_HEADS * HEAD_DIM
OFF_BV = OFF_BK + 2 * B_HEADS * HEAD_DIM
OFF_CQ = OFF_BV + B_WIDTH
OFF_CK = OFF_CQ + C_WIDTH
OFF_CV = OFF_CK + C_WIDTH
OFF_IQ = OFF_CV + C_WIDTH
OFF_IK = OFF_IQ + IDX_HEADS * IDX_DIM
IN_TOTAL = OFF_IK + IDX_DIM + IDX_HEADS
IN_PADDED = OFF_IK + LANES

_MXU_DTYPE = jnp.bfloat16
_NT = (((1,), (1,)), ((), ()))


def _params(sem, **flags):
    return pltpu.CompilerParams(dimension_semantics=sem, vmem_limit_bytes=VMEM_LIMIT, flags=flags or None)


def _resident(shape):
    return pl.BlockSpec(shape, lambda *_: (0,) * len(shape), pipeline_mode=pl.Buffered(1))


def _split_heads(t, even):
    zero = jnp.zeros((), t.dtype)
    return jnp.where(even, t, zero), jnp.where(even, zero, t)


def _rope_table_kernel(pos_ref, inv_ref, sgn_ref, cos_ref, sin_ref):
    ang = pos_ref[...].astype(jnp.float32) * inv_ref[...]
    cos_ref[...] = jnp.cos(ang)
    sin_ref[...] = jnp.sin(ang) * sgn_ref[...]


def _rope_tables(positions):
    m = positions.size
    tm = min(m, 2048)
    inv = ROPE_THETA ** (-jnp.arange(0, HEAD_DIM, 2, dtype=jnp.float32) / HEAD_DIM)
    inv = jnp.tile(inv, LANES // (HEAD_DIM // 2))[None, :]
    sgn = jnp.tile(jnp.concatenate([-jnp.ones(HEAD_DIM // 2), jnp.ones(HEAD_DIM // 2)]),
                   LANES // HEAD_DIM).astype(jnp.float32)[None, :]
    row = pl.BlockSpec((tm, LANES), lambda i: (i, 0))
    const = pl.BlockSpec((1, LANES), lambda i: (0, 0))
    return pl.pallas_call(
        _rope_table_kernel,
        grid=(m // tm,),
        in_specs=[pl.BlockSpec((tm, 1), lambda i: (i, 0)), const, const],
        out_specs=[row, row],
        out_shape=[jax.ShapeDtypeStruct((m, LANES), jnp.float32)] * 2,
        compiler_params=_params(("parallel",)),
        name="rope_tables",
    )(positions.reshape(m, 1), inv, sgn)


def _in_proj_kernel(x_ref, w_ref, cos_ref, sin_ref,
                    a_ref, bq_ref, bk_ref, bvt_ref, cq_ref, ck_ref, cvt_ref,
                    iq_ref, ik2_ref, iwt_ref, *, key_tile):
    tm = x_ref.shape[0]
    xb = x_ref[...].astype(_MXU_DTYPE)
    cos = cos_ref[...]
    sin = sin_ref[...]
    lane = lax.broadcasted_iota(jnp.int32, (tm, LANES), 1)
    low_half = (lane & (HEAD_DIM - 1)) < HEAD_DIM // 2
    qk_scale = HEAD_DIM ** -0.5 * LOG2_E

    def proj(c0, n):
        return jnp.dot(xb, w_ref[:, c0:c0 + n], preferred_element_type=jnp.float32)

    def rope(t):
        rot = jnp.where(low_half, pltpu.roll(t, LANES - HEAD_DIM // 2, 1),
                        pltpu.roll(t, HEAD_DIM // 2, 1))
        return t * cos + rot * sin

    def store(ref, h, roped=False, scale=None, col0=0):
        for j in range(h.shape[1] // LANES):
            t = h[:, j * LANES:(j + 1) * LANES]
            if roped:
                t = rope(t)
            if scale is not None:
                t = t * scale
            ref[:, col0 + j * LANES:col0 + (j + 1) * LANES] = t.astype(ref.dtype)

    store(a_ref, proj(OFF_A, A_WIDTH), scale=qk_scale)
    store(a_ref, proj(OFF_A + A_WIDTH, 2 * A_WIDTH), col0=A_WIDTH)
    store(bq_ref, proj(OFF_BQ, OFF_BK - OFF_BQ), roped=True, scale=qk_scale)
    store(bk_ref, proj(OFF_BK, OFF_BV - OFF_BK), roped=True)
    bvt_ref[0] = proj(OFF_BV, B_WIDTH).T.astype(bvt_ref.dtype)
    store(cq_ref, proj(OFF_CQ, C_WIDTH), roped=True, scale=qk_scale)
    store(ck_ref, proj(OFF_CK, C_WIDTH), roped=True)
    cv = proj(OFF_CV, C_WIDTH)
    for c in range(tm // key_tile):
        cvt_ref[c] = cv[c * key_tile:(c + 1) * key_tile, :].T.astype(cvt_ref.dtype)
    store(iq_ref, proj(OFF_IQ, IDX_HEADS * IDX_DIM), roped=True)
    last = proj(OFF_IK, LANES)
    ikr = rope(last)
    ik2 = jnp.where(lane < IDX_DIM, ikr, pltpu.roll(ikr, IDX_DIM, 1))
    ik2_ref[...] = ik2.astype(ik2_ref.dtype)
    iwt_ref[...] = last.T[IDX_DIM:IDX_DIM + IDX_HEADS, :] * ((IDX_HEADS * IDX_DIM) ** -0.5)


def _in_proj(x, w, cos, sin, bsz, seq, key_tile):
    m, d = x.shape
    tm = min(seq, 512)
    n_s = seq // tm
    f = _MXU_DTYPE

    def rows(width):
        return pl.BlockSpec((tm, width), lambda i: (i, 0))

    out_shape = [
        jax.ShapeDtypeStruct((m, 3 * A_WIDTH), f),
        jax.ShapeDtypeStruct((m, OFF_BK - OFF_BQ), f),
        jax.ShapeDtypeStruct((m, OFF_BV - OFF_BK), f),
        jax.ShapeDtypeStruct((bsz, n_s, B_WIDTH, tm), f),
        jax.ShapeDtypeStruct((m, C_WIDTH), f),
        jax.ShapeDtypeStruct((m, C_WIDTH), f),
        jax.ShapeDtypeStruct((bsz, seq // key_tile, C_WIDTH, key_tile), f),
        jax.ShapeDtypeStruct((m, IDX_HEADS * IDX_DIM), f),
        jax.ShapeDtypeStruct((m, LANES), f),
        jax.ShapeDtypeStruct((bsz, IDX_HEADS, seq), jnp.float32),
    ]
    out_specs = [
        rows(3 * A_WIDTH), rows(OFF_BK - OFF_BQ), rows(OFF_BV - OFF_BK),
        pl.BlockSpec((None, 1, B_WIDTH, tm), lambda i: (i // n_s, i % n_s, 0, 0)),
        rows(C_WIDTH), rows(C_WIDTH),
        pl.BlockSpec((None, tm // key_tile, C_WIDTH, key_tile), lambda i: (i // n_s, i % n_s, 0, 0)),
        rows(IDX_HEADS * IDX_DIM), rows(LANES),
        pl.BlockSpec((None, IDX_HEADS, tm), lambda i: (i // n_s, 0, i % n_s)),
    ]
    return pl.pallas_call(
        functools.partial(_in_proj_kernel, key_tile=key_tile),
        grid=(m // tm,),
        in_specs=[rows(d), _resident(w.shape), rows(LANES), rows(LANES)],
        out_specs=out_specs,
        out_shape=out_shape,
        compiler_params=_params(("parallel",)),
        name="in_proj",
    )(x, w, cos, sin)


A_WINDOW_TILES = (A_LEFT_CHUNKS * CHUNK) // Q_BLOCK + 1


def _band_bias(rel_bias):
    width = A_WINDOW_TILES * Q_BLOCK
    span = Q_BLOCK + width - 1
    d = np.arange(span) - (width - 1) + A_LEFT_CHUNKS * CHUNK
    gen = rel_bias[:, np.clip(d, -REL_CLIP, REL_CLIP) + REL_CLIP].astype(jnp.float32)
    sheared = jnp.tile(gen, (1, Q_BLOCK + 1))[:, :Q_BLOCK * (span + 1)].reshape(-1, Q_BLOCK, span + 1)
    bias = sheared[:, :, :width][:, :, ::-1]
    r = np.arange(Q_BLOCK)[:, None]
    e = np.arange(width)[None, :]
    back = r // CHUNK + A_LEFT_CHUNKS - e // CHUNK
    in_band = (back >= 0) & (back <= A_LEFT_CHUNKS)
    return jnp.where(jnp.asarray(in_band)[None], bias * LOG2_E, NEG_INF)


def _band_kernel(q_ref, k_ref, v_ref, bias_ref, o_ref):
    i = pl.program_id(1)
    lane = lax.broadcasted_iota(jnp.int32, (Q_BLOCK, LANES), 1)
    even_head = lane < HEAD_DIM
    for pair in range(A_HEADS // 2):
        cols = slice(pair * LANES, (pair + 1) * LANES)
        qs = jnp.concatenate(_split_heads(q_ref[:, cols], even_head), axis=0)
        s_tiles = []
        for tt in range(A_WINDOW_TILES):
            t = i - (A_WINDOW_TILES - 1) + tt
            k0 = pl.multiple_of(jnp.maximum(t, 0) * Q_BLOCK, Q_BLOCK)
            kt = k_ref[pl.ds(k0, Q_BLOCK), cols]
            s = lax.dot_general(qs, kt, _NT, preferred_element_type=jnp.float32)
            lanes_tt = slice(tt * Q_BLOCK, (tt + 1) * Q_BLOCK)
            b = jnp.concatenate([bias_ref[2 * pair, :, lanes_tt], bias_ref[2 * pair + 1, :, lanes_tt]], axis=0)
            s_tiles.append(jnp.where(t >= 0, s + b, NEG_INF))
        m = functools.reduce(jnp.maximum, [jnp.max(s, axis=1, keepdims=True) for s in s_tiles])
        p_tiles = [jnp.exp2(s - m) for s in s_tiles]
        l = functools.reduce(jnp.add, [jnp.sum(p, axis=1, keepdims=True) for p in p_tiles])
        acc = jnp.zeros((2 * Q_BLOCK, LANES), jnp.float32)
        for tt in range(A_WINDOW_TILES):
            t = i - (A_WINDOW_TILES - 1) + tt
            k0 = pl.multiple_of(jnp.maximum(t, 0) * Q_BLOCK, Q_BLOCK)
            vt = v_ref[pl.ds(k0, Q_BLOCK), cols]
            acc = acc + jnp.dot(p_tiles[tt].astype(_MXU_DTYPE), vt, preferred_element_type=jnp.float32)
        acc = acc * (1.0 / l)
        o_ref[:, cols] = jnp.where(even_head, acc[:Q_BLOCK], acc[Q_BLOCK:]).astype(o_ref.dtype)


def _band_attention(a_qkv, bias, bsz, seq):
    m = a_qkv.shape[0]
    nq = seq // Q_BLOCK
    return pl.pallas_call(
        _band_kernel,
        grid=(bsz, nq),
        in_specs=[
            pl.BlockSpec((Q_BLOCK, A_WIDTH), lambda b, i: (b * nq + i, 0)),
            pl.BlockSpec((seq, A_WIDTH), lambda b, i: (b, 1)),
            pl.BlockSpec((seq, A_WIDTH), lambda b, i: (b, 2)),
            _resident(bias.shape),
        ],
        out_specs=pl.BlockSpec((Q_BLOCK, A_WIDTH), lambda b, i: (b * nq + i, 0)),
        out_shape=jax.ShapeDtypeStruct((m, A_WIDTH), _MXU_DTYPE),
        compiler_params=_params(("parallel", "arbitrary")),
        name="band_attention",
    )(a_qkv, a_qkv, a_qkv, bias)


def _diff_kernel(lam_ref, g_ref, q_ref, k_ref, vt_ref, o_ref, qs_ref, acc_ref, s_ref, p_ref,
                 *, lam_init, key_tile):
    i = pl.program_id(1)
    tk = key_tile
    bq = B_Q_BLOCK
    nq2 = 2 * bq
    lv = lam_ref[...]
    lam = (jnp.exp(jnp.sum(lv[0:1] * lv[1:2], axis=1, keepdims=True))
           - jnp.exp(jnp.sum(lv[2:3] * lv[3:4], axis=1, keepdims=True)) + lam_init)
    first_map = lax.broadcasted_iota(jnp.int32, (bq, LANES), 1) < HEAD_DIM
    for h in range(B_HEADS):
        qs_ref[h] = jnp.concatenate(_split_heads(q_ref[:, h * LANES:(h + 1) * LANES], first_map), axis=0)
    acc_ref[...] = jnp.zeros(acc_ref.shape, jnp.float32)
    col = lax.broadcasted_iota(jnp.int32, (1, nq2), 1)
    limit = i * bq + ((col & (bq - 1)) // CHUNK + 1) * CHUNK
    n_tiles = ((i + 1) * bq + tk - 1) // tk

    def fold(x, op):
        return op(x.reshape(tk // FOLD_ROWS, FOLD_ROWS, nq2), axis=0)

    def score_dots(t):
        k0 = pl.multiple_of(t * tk, tk)
        for h in range(B_HEADS):
            s_ref[h] = lax.dot_general(k_ref[pl.ds(k0, tk), h * LANES:(h + 1) * LANES], qs_ref[h], _NT,
                                       preferred_element_type=jnp.float32)

    def softmax_update(t, stats, masked):
        new_stats, weights = [], []
        for h in range(B_HEADS):
            s = s_ref[h]
            if masked:
                s = jnp.where(t * tk + lax.broadcasted_iota(jnp.int32, s.shape, 0) < limit, s, NEG_INF)
            m_old, l_old = stats[h]
            m_new = jnp.maximum(m_old, jnp.max(fold(s, jnp.max), axis=0, keepdims=True))
            alpha = jnp.exp2(m_old - m_new)
            p = jnp.exp2(s - m_new)
            new_stats.append((m_new, alpha * l_old + jnp.sum(fold(p, jnp.sum), axis=0, keepdims=True)))
            p_ref[h] = p.astype(_MXU_DTYPE)
            weights.append(alpha)
        return tuple(new_stats), weights

    def value_dots(t, weights):
        for h in range(B_HEADS):
            acc_ref[h] = weights[h] * acc_ref[h] + jnp.dot(vt_ref[t, h * B_V_DIM:(h + 1) * B_V_DIM, :], p_ref[h],
                                                      preferred_element_type=jnp.float32)

    def step(t, stats):
        stats, weights = softmax_update(t, stats, masked=False)
        score_dots(t + 1)
        value_dots(t, weights)
        return stats

    stats = tuple((jnp.full((1, nq2), NEG_INF, jnp.float32), jnp.zeros((1, nq2), jnp.float32))
                  for _ in range(B_HEADS))
    score_dots(0)
    stats = lax.fori_loop(0, n_tiles - 1, step, stats)
    stats, weights = softmax_update(n_tiles - 1, stats, masked=True)
    value_dots(n_tiles - 1, weights)

    for h in range(B_HEADS):
        acc = acc_ref[h] * (1.0 / stats[h][1])
        o = (acc[:, :bq] - acc[:, bq:] * lam).T
        ms = jnp.mean(o * o, axis=1, keepdims=True)
        o = o * lax.rsqrt(ms + NORM_EPS) * g_ref[...] * (1.0 - lam_init)
        o_ref[:, h * B_V_DIM:(h + 1) * B_V_DIM] = o.astype(o_ref.dtype)


def _diff_attention(bq, bk, bvt, lam_vecs, gain, lam_init, bsz, seq):
    m = bq.shape[0]
    nq = seq // B_Q_BLOCK
    n_kt, key_tile = bvt.shape[1], bvt.shape[3]
    return pl.pallas_call(
        functools.partial(_diff_kernel, lam_init=lam_init, key_tile=key_tile),
        grid=(bsz, nq),
        in_specs=[
            _resident(lam_vecs.shape),
            _resident(gain.shape),
            pl.BlockSpec((B_Q_BLOCK, B_HEADS * LANES), lambda b, i: (b * nq + i, 0)),
            pl.BlockSpec((seq, B_HEADS * LANES), lambda b, i: (b, 0)),
            pl.BlockSpec((None, n_kt, B_WIDTH, key_tile), lambda b, i: (b, 0, 0, 0)),
        ],
        out_specs=pl.BlockSpec((B_Q_BLOCK, B_WIDTH), lambda b, i: (b * nq + i, 0)),
        out_shape=jax.ShapeDtypeStruct((m, B_WIDTH), _MXU_DTYPE),
        scratch_shapes=[
            pltpu.VMEM((B_HEADS, 2 * B_Q_BLOCK, LANES), _MXU_DTYPE),
            pltpu.VMEM((B_HEADS, B_V_DIM, 2 * B_Q_BLOCK), jnp.float32),
            pltpu.VMEM((B_HEADS, key_tile, 2 * B_Q_BLOCK), jnp.float32),
            pltpu.VMEM((B_HEADS, key_tile, 2 * B_Q_BLOCK), _MXU_DTYPE),
        ],
        compiler_params=_params(("parallel", "arbitrary")),
        name="diff_attention",
    )(lam_vecs, gain, bq, bk, bvt)


C_BISECT_STEPS = 14


def _dsa_kernel(ik_ref, iq_ref, iw_ref, ck_ref, cvt_ref, cq_ref, o_ref,
                sc_ref, iqz_ref, qz_ref, acc_ref, s_ref, *, topk, key_tile, sel_tile, seq):
    i = pl.program_id(1)
    tk = key_tile
    cq = C_Q_BLOCK
    n_keys = (i + 1) * cq
    n_tiles = (n_keys + tk - 1) // tk
    n_sel = (n_keys + sel_tile - 1) // sel_tile
    lane_q = lax.broadcasted_iota(jnp.int32, (1, cq), 1)
    limit = i * cq + (lane_q // CHUNK + 1) * CHUNK
    searched = limit > topk
    even_head = lax.broadcasted_iota(jnp.int32, (cq, LANES), 1) < HEAD_DIM
    kk = float(topk)

    def fold(x, op=jnp.sum):
        return op(x.reshape(x.shape[0] // FOLD_ROWS, FOLD_ROWS, cq), axis=0)

    def key_pos(k0, rows):
        return k0 + lax.broadcasted_iota(jnp.int32, (rows, cq), 0)

    for j in range(IDX_HEADS // 2):
        iqz_ref[2 * j], iqz_ref[2 * j + 1] = _split_heads(iq_ref[:, j * LANES:(j + 1) * LANES], even_head)
    for j in range(C_HEADS // 2):
        qz_ref[2 * j], qz_ref[2 * j + 1] = _split_heads(cq_ref[:, j * LANES:(j + 1) * LANES], even_head)
    w = iw_ref[...]

    def index_scores(t, c):
        k0 = pl.multiple_of(t * tk, tk)
        ikt = ik_ref[pl.ds(k0, tk), :]
        acc = jnp.zeros((tk, cq), jnp.float32)
        for h in range(IDX_HEADS):
            logit = lax.dot_general(ikt, iqz_ref[h], _NT, preferred_element_type=jnp.float32)
            acc = acc + jnp.maximum(logit, 0.0) * w[h:h + 1, :]
        adm = key_pos(k0, tk) < limit
        sc_ref[pl.ds(k0, tk), :] = jnp.where(adm, acc, NEG_INF)
        return (jnp.minimum(c[0], fold(jnp.where(adm, acc, jnp.inf), jnp.min)),
                jnp.maximum(c[1], fold(jnp.where(adm, acc, NEG_INF), jnp.max)))

    lo8, hi8 = lax.fori_loop(0, n_tiles, index_scores,
                             (jnp.full((FOLD_ROWS, cq), jnp.inf, jnp.float32),
                              jnp.full((FOLD_ROWS, cq), NEG_INF, jnp.float32)))

    @pl.when(n_tiles * tk < n_sel * sel_tile)
    def _():
        k0 = pl.multiple_of(n_tiles * tk, tk)
        sc_ref[pl.ds(k0, tk), :] = jnp.full((tk, cq), NEG_INF, jnp.float32)

    def over_scores(fn, init):
        def body(t, acc):
            k0 = pl.multiple_of(t * sel_tile, sel_tile)
            return fn(sc_ref[pl.ds(k0, sel_tile), :], k0, acc)
        return lax.fori_loop(0, n_sel, body, init)

    def count(pred):
        part = over_scores(lambda x, k0, acc: acc + fold(jnp.where(pred(x, k0), 1.0, 0.0)),
                           jnp.zeros((FOLD_ROWS, cq), jnp.float32))
        return jnp.sum(part, axis=0, keepdims=True)

    def count_ge(thr):
        return count(lambda x, k0: x >= thr)

    def max_where(pred):
        part = over_scores(lambda x, k0, acc: jnp.maximum(acc, fold(jnp.where(pred(x), x, NEG_INF), jnp.max)),
                           jnp.full((FOLD_ROWS, cq), NEG_INF, jnp.float32))
        return jnp.max(part, axis=0, keepdims=True)

    def write_mask(keep_fn):
        def body(t, carry):
            k0 = pl.multiple_of(t * tk, tk)
            x = sc_ref[pl.ds(k0, tk), :]
            kpos = key_pos(k0, tk)
            keep = jnp.logical_and(keep_fn(x, kpos), kpos < limit)
            sc_ref[pl.ds(k0, tk), :] = jnp.where(keep, 0.0, NEG_INF)
            return carry
        lax.fori_loop(0, n_tiles, body, 0)

    @pl.when(n_keys <= topk)
    def _():
        write_mask(lambda x, kpos: kpos >= 0)

    @pl.when(n_keys > topk)
    def _():
        lo = jnp.min(lo8, axis=0, keepdims=True)
        hi = jnp.max(hi8, axis=0, keepdims=True)

        def bisect(_, c):
            lo, hi = c
            mid = lo + (hi - lo) * 0.5
            ok = count_ge(mid) >= kk
            return jnp.where(ok, mid, lo), jnp.where(ok, hi, mid)

        lo, hi = lax.fori_loop(0, C_BISECT_STEPS, bisect, (lo, hi))

        def settled(cnt):
            return jnp.logical_or(cnt >= kk, jnp.logical_not(searched))

        thr = max_where(lambda x: x <= hi)
        cnt = count_ge(thr)

        def unsettled(c):
            return jnp.max(jnp.where(settled(c[1]), 0, 1)) > 0

        def walk(c):
            thr, cnt = c
            nxt = max_where(lambda x: x < thr)
            ncnt = count_ge(nxt)
            stay = settled(cnt)
            return jnp.where(stay, thr, nxt), jnp.where(stay, cnt, ncnt)

        thr, cnt = lax.while_loop(unsettled, walk, (thr, cnt))
        thr = jnp.where(searched, thr, NEG_INF)
        has_ties = jnp.max(jnp.where(jnp.logical_and(searched, cnt > kk), 1, 0)) > 0

        @pl.when(jnp.logical_not(has_ties))
        def _():
            write_mask(lambda x, kpos: x >= thr)

        @pl.when(has_ties)
        def _():
            need = kk - count(lambda x, k0: x > thr)
            cut = jnp.zeros((1, cq), jnp.int32)
            bit = seq
            while bit >= 1:
                cand = cut + bit
                before = count(lambda x, k0: jnp.logical_and(x == thr, key_pos(k0, sel_tile) < cand))
                cut = jnp.where(before <= need, cand, cut)
                bit //= 2
            cut = jnp.where(searched, cut, 2 * seq)
            write_mask(lambda x, kpos: jnp.logical_or(x > thr, jnp.logical_and(x == thr, kpos < cut)))

    acc_ref[...] = jnp.zeros(acc_ref.shape, jnp.float32)

    def score_dots(t):
        k0 = pl.multiple_of(t * tk, tk)
        for h in range(C_HEADS):
            s_ref[h] = lax.dot_general(ck_ref[pl.ds(k0, tk), (h // 2) * LANES:(h // 2 + 1) * LANES], qz_ref[h], _NT,
                                       preferred_element_type=jnp.float32)

    def softmax_update(t, stats):
        mask = sc_ref[pl.ds(pl.multiple_of(t * tk, tk), tk), :]
        new_stats, weights = [], []
        for h in range(C_HEADS):
            m_old, l_old = stats[h]
            s = s_ref[h] + mask
            m_new = jnp.maximum(m_old, jnp.max(fold(s, jnp.max), axis=0, keepdims=True))
            alpha = jnp.exp2(m_old - m_new)
            p = jnp.exp2(s - m_new)
            new_stats.append((m_new, alpha * l_old + jnp.sum(fold(p), axis=0, keepdims=True)))
            weights.append((alpha, p.astype(_MXU_DTYPE)))
        return tuple(new_stats), weights

    def value_dots(t, weights):
        for h in range(C_HEADS):
            alpha, p = weights[h]
            acc_ref[h] = alpha * acc_ref[h] + jnp.dot(cvt_ref[t, h * HEAD_DIM:(h + 1) * HEAD_DIM, :], p,
                                                      preferred_element_type=jnp.float32)

    def step(t, stats):
        stats, weights = softmax_update(t, stats)
        score_dots(t + 1)
        value_dots(t, weights)
        return stats

    stats = tuple((jnp.full((1, cq), NEG_INF, jnp.float32), jnp.zeros((1, cq), jnp.float32))
                  for _ in range(C_HEADS))
    score_dots(0)
    stats = lax.fori_loop(0, n_tiles - 1, step, stats)
    stats, weights = softmax_update(n_tiles - 1, stats)
    value_dots(n_tiles - 1, weights)
    out = jnp.concatenate([acc_ref[h] * (1.0 / stats[h][1]) for h in range(C_HEADS)], axis=0)
    o_ref[...] = out.T.astype(o_ref.dtype)


def _dsa_attention(ik2, iq, iwt, ck, cvt, cq, bsz, seq, key_tile):
    m = iq.shape[0]
    cqb = C_Q_BLOCK
    nq = seq // cqb
    topk = min(C_TOPK_MAX, seq // 4)
    sel_tile = min(seq, 512)
    assert key_tile == cqb and seq % sel_tile == 0 and sel_tile % key_tile == 0
    return pl.pallas_call(
        functools.partial(_dsa_kernel, topk=topk, key_tile=key_tile, sel_tile=sel_tile, seq=seq),
        grid=(bsz, nq),
        in_specs=[
            pl.BlockSpec((seq, LANES), lambda b, i: (b, 0)),
            pl.BlockSpec((cqb, IDX_HEADS * IDX_DIM), lambda b, i: (b * nq + i, 0)),
            pl.BlockSpec((None, IDX_HEADS, cqb), lambda b, i: (b, 0, i)),
            pl.BlockSpec((seq, C_WIDTH), lambda b, i: (b, 0)),
            pl.BlockSpec((None, seq // key_tile, C_WIDTH, key_tile), lambda b, i: (b, 0, 0, 0)),
            pl.BlockSpec((cqb, C_WIDTH), lambda b, i: (b * nq + i, 0)),
        ],
        out_specs=pl.BlockSpec((cqb, C_WIDTH), lambda b, i: (b * nq + i, 0)),
        out_shape=jax.ShapeDtypeStruct((m, C_WIDTH), _MXU_DTYPE),
        scratch_shapes=[
            pltpu.VMEM((seq, cqb), jnp.float32),
            pltpu.VMEM((IDX_HEADS, cqb, LANES), _MXU_DTYPE),
            pltpu.VMEM((C_HEADS, cqb, LANES), _MXU_DTYPE),
            pltpu.VMEM((C_HEADS, HEAD_DIM, cqb), jnp.float32),
            pltpu.VMEM((C_HEADS, key_tile, cqb), jnp.float32),
        ],
        compiler_params=_params(("parallel", "arbitrary")),
        name="dsa_attention",
    )(ik2, iq, iwt, ck, cvt, cq)


def _layer_norm(z, g, b):
    mu = jnp.mean(z, axis=-1, keepdims=True)
    zc = z - mu
    var = jnp.mean(zc * zc, axis=-1, keepdims=True)
    return zc * lax.rsqrt(var + NORM_EPS) * g + b


def _post_kernel(x_ref, oa_ref, ob_ref, oc_ref, p_ref, wo_ref, wup_ref, wdown_ref, wgate_ref, wple_ref,
                 g1_ref, b1_ref, g2_ref, b2_ref, y_ref, *, alpha, ff_tile):
    y = jnp.dot(oa_ref[...], wo_ref[0:A_WIDTH, :], preferred_element_type=jnp.float32)
    y = y + jnp.dot(ob_ref[...], wo_ref[A_WIDTH:A_WIDTH + B_WIDTH, :], preferred_element_type=jnp.float32)
    y = y + jnp.dot(oc_ref[...], wo_ref[A_WIDTH + B_WIDTH:, :], preferred_element_type=jnp.float32)
    x1 = _layer_norm(alpha * x_ref[...] + y, g1_ref[...], b1_ref[...])
    x1b = x1.astype(_MXU_DTYPE)
    ff = jnp.zeros(x1.shape, jnp.float32)
    for f0 in range(0, wup_ref.shape[1], ff_tile):
        u = jnp.dot(x1b, wup_ref[:, f0:f0 + ff_tile], preferred_element_type=jnp.float32)
        u = jnp.square(jnp.maximum(u, 0.0)).astype(_MXU_DTYPE)
        ff = ff + jnp.dot(u, wdown_ref[f0:f0 + ff_tile, :], preferred_element_type=jnp.float32)
    gate = jax.nn.sigmoid(jnp.dot(x1b, wgate_ref[...], preferred_element_type=jnp.float32))
    ple = gate * jnp.dot(p_ref[...].astype(_MXU_DTYPE), wple_ref[...], preferred_element_type=jnp.float32)
    y_ref[...] = _layer_norm(alpha * x1 + ff + ple, g2_ref[...], b2_ref[...])


def _post(x, oa, ob, oc, p, wo, wup, wdown, wgate, wple, g1, b1, g2, b2, alpha, seq):
    m, d = x.shape
    tm = min(seq, 512)

    def rows(width):
        return pl.BlockSpec((tm, width), lambda i: (i, 0))

    weights = [wo, wup, wdown, wgate, wple, g1, b1, g2, b2]
    return pl.pallas_call(
        functools.partial(_post_kernel, alpha=alpha, ff_tile=512),
        grid=(m // tm,),
        in_specs=[rows(d), rows(A_WIDTH), rows(B_WIDTH), rows(C_WIDTH), rows(p.shape[1])]
                 + [_resident(t.shape) for t in weights],
        out_specs=rows(d),
        out_shape=jax.ShapeDtypeStruct((m, d), jnp.float32),
        compiler_params=_params(("parallel",)),
        name="post_mlp",
    )(x, oa, ob, oc, p, *weights)


def kernel(x, p, positions, w_in, rel_bias, lam_q1, lam_k1, lam_q2, lam_k2, diff_norm_g, w_o,
           ln1_g, ln1_b, w_up, w_down, w_ple_gate, w_ple, ln2_g, ln2_b):
    bsz, seq, d_model = x.shape
    depth = w_in.shape[0]
    m = bsz * seq
    assert seq % C_Q_BLOCK == 0 and w_in.shape[2] == IN_TOTAL
    key_tile = C_Q_BLOCK
    alpha = (2 * depth) ** 0.25
    cast = lambda t: t.astype(_MXU_DTYPE)

    cos, sin = _rope_tables(positions)
    xf = x.reshape(m, d_model)
    for i in range(depth):
        w = jnp.pad(cast(w_in[i]), ((0, 0), (0, IN_PADDED - IN_TOTAL)))
        a_qkv, bq, bk, bvt, cq, ck, cvt, iq, ik2, iwt = _in_proj(xf, w, cos, sin, bsz, seq, key_tile)
        lam_init = 0.8 - 0.6 * math.exp(-0.3 * i)
        lam_vecs = jnp.stack([lam_q1[i], lam_k1[i], lam_q2[i], lam_k2[i]]).astype(jnp.float32)
        o_a = _band_attention(a_qkv, _band_bias(rel_bias[i]), bsz, seq)
        o_b = _diff_attention(bq, bk, bvt, lam_vecs, diff_norm_g[i][None, :].astype(jnp.float32),
                              lam_init, bsz, seq)
        o_c = _dsa_attention(ik2, iq, iwt, ck, cvt, cq, bsz, seq, key_tile)
        row = lambda t: t[i][None, :].astype(jnp.float32)
        xf = _post(xf, o_a, o_b, o_c, p[i].reshape(m, p.shape[-1]),
                   cast(w_o[i]), cast(w_up[i]), cast(w_down[i]), cast(w_ple_gate[i]), cast(w_ple[i]),
                   row(ln1_g), row(ln1_b), row(ln2_g), row(ln2_b), alpha, seq)
    return xf.reshape(bsz, seq, d_model)
```

```python
import functools
import math

import jax
import jax.numpy as jnp
import numpy as np
from jax import lax
from jax.experimental import pallas as pl
from jax.experimental.pallas import tpu as pltpu

CHUNK = 64
HEAD_DIM = 64
A_HEADS = 4
A_LEFT_CHUNKS = 8
REL_CLIP = 128
B_HEADS = 4
B_V_DIM = 2 * HEAD_DIM
C_HEADS = 4
C_TOPK_MAX = 256
IDX_HEADS = 8
IDX_DIM = 64
A_WIDTH = A_HEADS * HEAD_DIM
B_WIDTH = B_HEADS * B_V_DIM
C_WIDTH = C_HEADS * HEAD_DIM
ROPE_THETA = 10000.0
NORM_EPS = 1e-5
NEG_INF = -1e30
LOG2_E = math.log2(math.e)

LANES = 128
FOLD_ROWS = 64
Q_BLOCK = 128
B_Q_BLOCK = 256
C_Q_BLOCK = 256
VMEM_LIMIT = 56 * 1024 * 1024

OFF_A = 0
OFF_BQ = 3 * A_WIDTH
OFF_BK = OFF_BQ + 2 * B_HEADS * HEAD_DIM
OFF_BV = OFF_BK + 2 * B_HEADS * HEAD_DIM
OFF_CQ = OFF_BV + B_WIDTH
OFF_CK = OFF_CQ + C_WIDTH
OFF_CV = OFF_CK + C_WIDTH
OFF_IQ = OFF_CV + C_WIDTH
OFF_IK = OFF_IQ + IDX_HEADS * IDX_DIM
IN_TOTAL = OFF_IK + IDX_DIM + IDX_HEADS
IN_PADDED = OFF_IK + LANES

_MXU_DTYPE = jnp.bfloat16
_NT = (((1,), (1,)), ((), ()))


def _params(sem, **flags):
    return pltpu.CompilerParams(dimension_semantics=sem, vmem_limit_bytes=VMEM_LIMIT, flags=flags or None)


def _resident(shape):
    return pl.BlockSpec(shape, lambda *_: (0,) * len(shape), pipeline_mode=pl.Buffered(1))


def _two_slot_pipeline(n_tiles, max_tile, produce, consume, carry):
    produce(0, 0)

    def pair(j, c):
        t0 = 2 * j
        produce(t0 + 1, 1)
        c = consume(t0, 0, c)
        produce(jnp.minimum(t0 + 2, max_tile), 0)
        return consume(t0 + 1, 1, c)

    carry = lax.fori_loop(0, n_tiles // 2, pair, carry)
    return lax.cond(n_tiles % 2 == 1, lambda c: consume(n_tiles - 1, 0, c), lambda c: c, carry)


def _split_heads(t, even):
    zero = jnp.zeros((), t.dtype)
    return jnp.where(even, t, zero), jnp.where(even, zero, t)


def _rope_table_kernel(pos_ref, inv_ref, sgn_ref, cos_ref, sin_ref):
    ang = pos_ref[...].astype(jnp.float32) * inv_ref[...]
    cos_ref[...] = jnp.cos(ang)
    sin_ref[...] = jnp.sin(ang) * sgn_ref[...]


def _rope_tables(positions):
    m = positions.size
    tm = min(m, 2048)
    inv = ROPE_THETA ** (-jnp.arange(0, HEAD_DIM, 2, dtype=jnp.float32) / HEAD_DIM)
    inv = jnp.tile(inv, LANES // (HEAD_DIM // 2))[None, :]
    sgn = jnp.tile(jnp.concatenate([-jnp.ones(HEAD_DIM // 2), jnp.ones(HEAD_DIM // 2)]),
                   LANES // HEAD_DIM).astype(jnp.float32)[None, :]
    row = pl.BlockSpec((tm, LANES), lambda i: (i, 0))
    const = pl.BlockSpec((1, LANES), lambda i: (0, 0))
    return pl.pallas_call(
        _rope_table_kernel,
        grid=(m // tm,),
        in_specs=[pl.BlockSpec((tm, 1), lambda i: (i, 0)), const, const],
        out_specs=[row, row],
        out_shape=[jax.ShapeDtypeStruct((m, LANES), jnp.float32)] * 2,
        compiler_params=_params(("parallel",)),
        name="rope_tables",
    )(positions.reshape(m, 1), inv, sgn)


def _in_proj_kernel(x_ref, w_ref, cos_ref, sin_ref,
                    a_ref, avt_ref, bq_ref, bk_ref, bvt_ref, cq_ref, ck_ref, cvt_ref,
                    iq_ref, ik2_ref, iwt_ref, *, key_tile):
    tm = x_ref.shape[0]
    xb = x_ref[...].astype(_MXU_DTYPE)
    cos = cos_ref[...]
    sin = sin_ref[...]
    lane = lax.broadcasted_iota(jnp.int32, (tm, LANES), 1)
    low_half = (lane & (HEAD_DIM - 1)) < HEAD_DIM // 2
    qk_scale = HEAD_DIM ** -0.5 * LOG2_E

    def proj(c0, n):
        return jnp.dot(xb, w_ref[:, c0:c0 + n], preferred_element_type=jnp.float32)

    def rope(t):
        rot = jnp.where(low_half, pltpu.roll(t, LANES - HEAD_DIM // 2, 1),
                        pltpu.roll(t, HEAD_DIM // 2, 1))
        return t * cos + rot * sin

    def store(ref, h, roped=False, scale=None, col0=0):
        for j in range(h.shape[1] // LANES):
            t = h[:, j * LANES:(j + 1) * LANES]
            if roped:
                t = rope(t)
            if scale is not None:
                t = t * scale
            ref[:, col0 + j * LANES:col0 + (j + 1) * LANES] = t.astype(ref.dtype)

    store(a_ref, proj(OFF_A, A_WIDTH), scale=qk_scale)
    store(a_ref, proj(OFF_A + A_WIDTH, A_WIDTH), col0=A_WIDTH)
    av = proj(OFF_A + 2 * A_WIDTH, A_WIDTH)
    for c in range(tm // Q_BLOCK):
        avt_ref[c] = av[c * Q_BLOCK:(c + 1) * Q_BLOCK, :].T.astype(avt_ref.dtype)
    store(bq_ref, proj(OFF_BQ, OFF_BK - OFF_BQ), roped=True, scale=qk_scale)
    store(bk_ref, proj(OFF_BK, OFF_BV - OFF_BK), roped=True)
    bvt_ref[0] = proj(OFF_BV, B_WIDTH).T.astype(bvt_ref.dtype)
    store(cq_ref, proj(OFF_CQ, C_WIDTH), roped=True, scale=qk_scale)
    store(ck_ref, proj(OFF_CK, C_WIDTH), roped=True)
    cv = proj(OFF_CV, C_WIDTH)
    for c in range(tm // key_tile):
        cvt_ref[c] = cv[c * key_tile:(c + 1) * key_tile, :].T.astype(cvt_ref.dtype)
    store(iq_ref, proj(OFF_IQ, IDX_HEADS * IDX_DIM), roped=True)
    last = proj(OFF_IK, LANES)
    ikr = rope(last)
    ik2 = jnp.where(lane < IDX_DIM, ikr, pltpu.roll(ikr, IDX_DIM, 1))
    ik2_ref[...] = ik2.astype(ik2_ref.dtype)
    iwt_ref[...] = last.T[IDX_DIM:IDX_DIM + IDX_HEADS, :] * ((IDX_HEADS * IDX_DIM) ** -0.5)


def _in_proj(x, w, cos, sin, bsz, seq, key_tile):
    m, d = x.shape
    tm = min(seq, 512)
    n_s = seq // tm
    f = _MXU_DTYPE

    def rows(width):
        return pl.BlockSpec((tm, width), lambda i: (i, 0))

    out_shape = [
        jax.ShapeDtypeStruct((m, 2 * A_WIDTH), f),
        jax.ShapeDtypeStruct((bsz, seq // Q_BLOCK, A_WIDTH, Q_BLOCK), f),
        jax.ShapeDtypeStruct((m, OFF_BK - OFF_BQ), f),
        jax.ShapeDtypeStruct((m, OFF_BV - OFF_BK), f),
        jax.ShapeDtypeStruct((bsz, n_s, B_WIDTH, tm), f),
        jax.ShapeDtypeStruct((m, C_WIDTH), f),
        jax.ShapeDtypeStruct((m, C_WIDTH), f),
        jax.ShapeDtypeStruct((bsz, seq // key_tile, C_WIDTH, key_tile), f),
        jax.ShapeDtypeStruct((m, IDX_HEADS * IDX_DIM), f),
        jax.ShapeDtypeStruct((m, LANES), f),
        jax.ShapeDtypeStruct((bsz, IDX_HEADS, seq), jnp.float32),
    ]
    out_specs = [
        rows(2 * A_WIDTH),
        pl.BlockSpec((None, tm // Q_BLOCK, A_WIDTH, Q_BLOCK), lambda i: (i // n_s, i % n_s, 0, 0)),
        rows(OFF_BK - OFF_BQ), rows(OFF_BV - OFF_BK),
        pl.BlockSpec((None, 1, B_WIDTH, tm), lambda i: (i // n_s, i % n_s, 0, 0)),
        rows(C_WIDTH), rows(C_WIDTH),
        pl.BlockSpec((None, tm // key_tile, C_WIDTH, key_tile), lambda i: (i // n_s, i % n_s, 0, 0)),
        rows(IDX_HEADS * IDX_DIM), rows(LANES),
        pl.BlockSpec((None, IDX_HEADS, tm), lambda i: (i // n_s, 0, i % n_s)),
    ]
    return pl.pallas_call(
        functools.partial(_in_proj_kernel, key_tile=key_tile),
        grid=(m // tm,),
        in_specs=[rows(d), _resident(w.shape), rows(LANES), rows(LANES)],
        out_specs=out_specs,
        out_shape=out_shape,
        compiler_params=_params(("parallel",)),
        name="in_proj",
    )(x, w, cos, sin)


A_WINDOW_TILES = (A_LEFT_CHUNKS * CHUNK) // Q_BLOCK + 1
A_BLOCKS_PER_STEP = 2


def _band_bias(rel_bias):
    width = A_WINDOW_TILES * Q_BLOCK
    span = Q_BLOCK + width - 1
    d = np.arange(span) - (width - 1) + A_LEFT_CHUNKS * CHUNK
    gen = rel_bias[:, np.clip(d, -REL_CLIP, REL_CLIP) + REL_CLIP].astype(jnp.float32)
    sheared = jnp.tile(gen, (1, Q_BLOCK + 1))[:, :Q_BLOCK * (span + 1)].reshape(-1, Q_BLOCK, span + 1)
    bias = sheared[:, :, :width][:, :, ::-1]
    r = np.arange(Q_BLOCK)[:, None]
    e = np.arange(width)[None, :]
    back = r // CHUNK + A_LEFT_CHUNKS - e // CHUNK
    in_band = (back >= 0) & (back <= A_LEFT_CHUNKS)
    bias = jnp.where(jnp.asarray(in_band)[None], bias * LOG2_E, NEG_INF)
    return bias.reshape(A_HEADS // 2, 2, Q_BLOCK, width).transpose(0, 3, 1, 2).reshape(A_HEADS // 2, width, 2 * Q_BLOCK)


def _band_kernel(q_ref, k_ref, vt_ref, bias_ref, o_ref):
    even_head = lax.broadcasted_iota(jnp.int32, (Q_BLOCK, LANES), 1) < HEAD_DIM
    problems = []
    for sub in range(A_BLOCKS_PER_STEP):
        i = pl.program_id(1) * A_BLOCKS_PER_STEP + sub
        rows = slice(sub * Q_BLOCK, (sub + 1) * Q_BLOCK)
        tiles = []
        for tt in range(A_WINDOW_TILES):
            t = i - (A_WINDOW_TILES - 1) + tt
            tiles.append((t, jnp.maximum(t, 0)))
        for pair in range(A_HEADS // 2):
            problems.append((rows, slice(pair * LANES, (pair + 1) * LANES), pair, tiles))
    scores = []
    for rows, cols, pair, tiles in problems:
        qs = jnp.concatenate(_split_heads(q_ref[rows, cols], even_head), axis=0)
        scores.append([lax.dot_general(k_ref[pl.ds(pl.multiple_of(tc * Q_BLOCK, Q_BLOCK), Q_BLOCK), cols],
                                       qs, _NT, preferred_element_type=jnp.float32)
                       for t, tc in tiles])
    probs = []
    for (rows, cols, pair, tiles), s_raw in zip(problems, scores):
        s_tiles = [jnp.where(t >= 0, s + bias_ref[pair, tt * Q_BLOCK:(tt + 1) * Q_BLOCK, :], NEG_INF)
                   for tt, ((t, tc), s) in enumerate(zip(tiles, s_raw))]
        m = jnp.max(functools.reduce(jnp.maximum, s_tiles), axis=0, keepdims=True)
        p_tiles = [jnp.exp2(s - m) for s in s_tiles]
        l = jnp.sum(functools.reduce(jnp.add, p_tiles), axis=0, keepdims=True)
        probs.append(([p.astype(_MXU_DTYPE) for p in p_tiles], l))
    for (rows, cols, pair, tiles), (p_tiles, l) in zip(problems, probs):
        acc = jnp.zeros((LANES, 2 * Q_BLOCK), jnp.float32)
        for (t, tc), p in zip(tiles, p_tiles):
            acc = acc + jnp.dot(vt_ref[tc, cols, :], p, preferred_element_type=jnp.float32)
        acc = acc * (1.0 / l)
        out_t = jnp.concatenate([acc[:HEAD_DIM, :Q_BLOCK], acc[HEAD_DIM:, Q_BLOCK:]], axis=0)
        o_ref[rows, cols] = out_t.T.astype(o_ref.dtype)


def _band_attention(a_qk, avt, bias, bsz, seq):
    m = a_qk.shape[0]
    step_rows = A_BLOCKS_PER_STEP * Q_BLOCK
    nq = seq // step_rows
    return pl.pallas_call(
        _band_kernel,
        grid=(bsz, nq),
        in_specs=[
            pl.BlockSpec((step_rows, A_WIDTH), lambda b, i: (b * nq + i, 0)),
            pl.BlockSpec((seq, A_WIDTH), lambda b, i: (b, 1)),
            pl.BlockSpec((None, seq // Q_BLOCK, A_WIDTH, Q_BLOCK), lambda b, i: (b, 0, 0, 0)),
            _resident(bias.shape),
        ],
        out_specs=pl.BlockSpec((step_rows, A_WIDTH), lambda b, i: (b * nq + i, 0)),
        out_shape=jax.ShapeDtypeStruct((m, A_WIDTH), _MXU_DTYPE),
        compiler_params=_params(("parallel", "arbitrary")),
        name="band_attention",
    )(a_qk, a_qk, avt, bias)


def _diff_kernel(lam_ref, g_ref, q_ref, k_ref, vt_ref, o_ref, qs_ref, acc_ref, s_ref, p_ref,
                 *, lam_init, key_tile):
    i = pl.program_id(1)
    tk = key_tile
    bq = B_Q_BLOCK
    nq2 = 2 * bq
    lv = lam_ref[...]
    lam = (jnp.exp(jnp.sum(lv[0:1] * lv[1:2], axis=1, keepdims=True))
           - jnp.exp(jnp.sum(lv[2:3] * lv[3:4], axis=1, keepdims=True)) + lam_init)
    first_map = lax.broadcasted_iota(jnp.int32, (bq, LANES), 1) < HEAD_DIM
    for h in range(B_HEADS):
        qs_ref[h] = jnp.concatenate(_split_heads(q_ref[:, h * LANES:(h + 1) * LANES], first_map), axis=0)
    acc_ref[...] = jnp.zeros(acc_ref.shape, jnp.float32)
    col = lax.broadcasted_iota(jnp.int32, (1, nq2), 1)
    limit = i * bq + ((col & (bq - 1)) // CHUNK + 1) * CHUNK
    n_tiles = ((i + 1) * bq + tk - 1) // tk

    def fold(x, op):
        return op(x.reshape(tk // FOLD_ROWS, FOLD_ROWS, nq2), axis=0)

    def score_dots(t):
        k0 = pl.multiple_of(t * tk, tk)
        for h in range(B_HEADS):
            s_ref[h] = lax.dot_general(k_ref[pl.ds(k0, tk), h * LANES:(h + 1) * LANES], qs_ref[h], _NT,
                                       preferred_element_type=jnp.float32)

    def softmax_update(t, stats, masked):
        new_stats, weights = [], []
        for h in range(B_HEADS):
            s = s_ref[h]
            if masked:
                s = jnp.where(t * tk + lax.broadcasted_iota(jnp.int32, s.shape, 0) < limit, s, NEG_INF)
            m_old, l_old = stats[h]
            m_new = jnp.maximum(m_old, jnp.max(fold(s, jnp.max), axis=0, keepdims=True))
            alpha = jnp.exp2(m_old - m_new)
            p = jnp.exp2(s - m_new)
            new_stats.append((m_new, alpha * l_old + jnp.sum(fold(p, jnp.sum), axis=0, keepdims=True)))
            p_ref[h] = p.astype(_MXU_DTYPE)
            weights.append(alpha)
        return tuple(new_stats), weights

    def value_dots(t, weights):
        for h in range(B_HEADS):
            acc_ref[h] = weights[h] * acc_ref[h] + jnp.dot(vt_ref[t, h * B_V_DIM:(h + 1) * B_V_DIM, :], p_ref[h],
                                                      preferred_element_type=jnp.float32)

    def step(t, stats):
        stats, weights = softmax_update(t, stats, masked=False)
        score_dots(t + 1)
        value_dots(t, weights)
        return stats

    stats = tuple((jnp.full((1, nq2), NEG_INF, jnp.float32), jnp.zeros((1, nq2), jnp.float32))
                  for _ in range(B_HEADS))
    score_dots(0)
    stats = lax.fori_loop(0, n_tiles - 1, step, stats)
    stats, weights = softmax_update(n_tiles - 1, stats, masked=True)
    value_dots(n_tiles - 1, weights)

    for h in range(B_HEADS):
        acc = acc_ref[h] * (1.0 / stats[h][1])
        o = (acc[:, :bq] - acc[:, bq:] * lam).T
        ms = jnp.mean(o * o, axis=1, keepdims=True)
        o = o * lax.rsqrt(ms + NORM_EPS) * g_ref[...] * (1.0 - lam_init)
        o_ref[:, h * B_V_DIM:(h + 1) * B_V_DIM] = o.astype(o_ref.dtype)


def _diff_attention(bq, bk, bvt, lam_vecs, gain, lam_init, bsz, seq):
    m = bq.shape[0]
    nq = seq // B_Q_BLOCK
    n_kt, key_tile = bvt.shape[1], bvt.shape[3]
    return pl.pallas_call(
        functools.partial(_diff_kernel, lam_init=lam_init, key_tile=key_tile),
        grid=(bsz, nq),
        in_specs=[
            _resident(lam_vecs.shape),
            _resident(gain.shape),
            pl.BlockSpec((B_Q_BLOCK, B_HEADS * LANES), lambda b, i: (b * nq + i, 0)),
            pl.BlockSpec((seq, B_HEADS * LANES), lambda b, i: (b, 0)),
            pl.BlockSpec((None, n_kt, B_WIDTH, key_tile), lambda b, i: (b, 0, 0, 0)),
        ],
        out_specs=pl.BlockSpec((B_Q_BLOCK, B_WIDTH), lambda b, i: (b * nq + i, 0)),
        out_shape=jax.ShapeDtypeStruct((m, B_WIDTH), _MXU_DTYPE),
        scratch_shapes=[
            pltpu.VMEM((B_HEADS, 2 * B_Q_BLOCK, LANES), _MXU_DTYPE),
            pltpu.VMEM((B_HEADS, B_V_DIM, 2 * B_Q_BLOCK), jnp.float32),
            pltpu.VMEM((B_HEADS, key_tile, 2 * B_Q_BLOCK), jnp.float32),
            pltpu.VMEM((B_HEADS, key_tile, 2 * B_Q_BLOCK), _MXU_DTYPE),
        ],
        compiler_params=_params(("parallel", "arbitrary")),
        name="diff_attention",
    )(lam_vecs, gain, bq, bk, bvt)


C_BISECT_STEPS = 14


def _dsa_kernel(ik_ref, iq_ref, iw_ref, ck_ref, cvt_ref, cq_ref, o_ref,
                sc_ref, iqz_ref, qz_ref, acc_ref, s_ref, lg_ref, *, topk, key_tile, sel_tile, seq):
    i = pl.program_id(1)
    tk = key_tile
    cq = C_Q_BLOCK
    n_keys = (i + 1) * cq
    n_tiles = (n_keys + tk - 1) // tk
    n_sel = (n_keys + sel_tile - 1) // sel_tile
    lane_q = lax.broadcasted_iota(jnp.int32, (1, cq), 1)
    limit = i * cq + (lane_q // CHUNK + 1) * CHUNK
    searched = limit > topk
    even_head = lax.broadcasted_iota(jnp.int32, (cq, LANES), 1) < HEAD_DIM
    kk = float(topk)

    def fold(x, op=jnp.sum):
        return op(x.reshape(x.shape[0] // FOLD_ROWS, FOLD_ROWS, cq), axis=0)

    def key_pos(k0, rows):
        return k0 + lax.broadcasted_iota(jnp.int32, (rows, cq), 0)

    for j in range(IDX_HEADS // 2):
        iqz_ref[2 * j], iqz_ref[2 * j + 1] = _split_heads(iq_ref[:, j * LANES:(j + 1) * LANES], even_head)
    for j in range(C_HEADS // 2):
        qz_ref[2 * j], qz_ref[2 * j + 1] = _split_heads(cq_ref[:, j * LANES:(j + 1) * LANES], even_head)
    w = iw_ref[...]

    def index_dots(t, slot):
        ikt = ik_ref[pl.ds(pl.multiple_of(t * tk, tk), tk), :]
        for h in range(IDX_HEADS):
            lg_ref[slot, h] = lax.dot_general(ikt, iqz_ref[h], _NT, preferred_element_type=jnp.float32)

    def index_combine(t, slot, c):
        k0 = pl.multiple_of(t * tk, tk)
        acc = jnp.zeros((tk, cq), jnp.float32)
        for h in range(IDX_HEADS):
            acc = acc + jnp.maximum(lg_ref[slot, h], 0.0) * w[h:h + 1, :]
        adm = key_pos(k0, tk) < limit
        sc_ref[pl.ds(k0, tk), :] = jnp.where(adm, acc, NEG_INF)
        return (jnp.minimum(c[0], fold(jnp.where(adm, acc, jnp.inf), jnp.min)),
                jnp.maximum(c[1], fold(jnp.where(adm, acc, NEG_INF), jnp.max)))

    lo8, hi8 = _two_slot_pipeline(n_tiles, seq // tk - 1, index_dots, index_combine,
                                  (jnp.full((FOLD_ROWS, cq), jnp.inf, jnp.float32),
                                   jnp.full((FOLD_ROWS, cq), NEG_INF, jnp.float32)))

    @pl.when(n_tiles * tk < n_sel * sel_tile)
    def _():
        k0 = pl.multiple_of(n_tiles * tk, tk)
        sc_ref[pl.ds(k0, tk), :] = jnp.full((tk, cq), NEG_INF, jnp.float32)

    def over_scores(fn, init):
        def body(t, acc):
            k0 = pl.multiple_of(t * sel_tile, sel_tile)
            return fn(sc_ref[pl.ds(k0, sel_tile), :], k0, acc)
        return lax.fori_loop(0, n_sel, body, init)

    def count(pred):
        def body(x, k0, acc):
            hit = pred(x, k0)
            for g in range(sel_tile // FOLD_ROWS):
                acc = jnp.where(hit[g * FOLD_ROWS:(g + 1) * FOLD_ROWS], acc + 1.0, acc)
            return acc
        part = over_scores(body, jnp.zeros((FOLD_ROWS, cq), jnp.float32))
        return jnp.sum(part, axis=0, keepdims=True)

    def count_ge(thr):
        return count(lambda x, k0: x >= thr)

    def max_where(pred):
        part = over_scores(lambda x, k0, acc: jnp.maximum(acc, fold(jnp.where(pred(x), x, NEG_INF), jnp.max)),
                           jnp.full((FOLD_ROWS, cq), NEG_INF, jnp.float32))
        return jnp.max(part, axis=0, keepdims=True)

    def write_mask(keep_fn):
        def body(t, carry):
            k0 = pl.multiple_of(t * tk, tk)
            x = sc_ref[pl.ds(k0, tk), :]
            kpos = key_pos(k0, tk)
            keep = jnp.logical_and(keep_fn(x, kpos), kpos < limit)
            sc_ref[pl.ds(k0, tk), :] = jnp.where(keep, 0.0, NEG_INF)
            return carry
        lax.fori_loop(0, n_tiles, body, 0)

    @pl.when(n_keys <= topk)
    def _():
        write_mask(lambda x, kpos: kpos >= 0)

    @pl.when(n_keys > topk)
    def _():
        lo = jnp.min(lo8, axis=0, keepdims=True)
        hi = jnp.max(hi8, axis=0, keepdims=True)

        def bisect(_, c):
            lo, hi = c
            mid = lo + (hi - lo) * 0.5
            ok = count_ge(mid) >= kk
            return jnp.where(ok, mid, lo), jnp.where(ok, hi, mid)

        lo, hi = lax.fori_loop(0, C_BISECT_STEPS, bisect, (lo, hi))

        def settled(cnt):
            return jnp.logical_or(cnt >= kk, jnp.logical_not(searched))

        thr = max_where(lambda x: x <= hi)
        cnt = count_ge(thr)

        def unsettled(c):
            return jnp.max(jnp.where(settled(c[1]), 0, 1)) > 0

        def walk(c):
            thr, cnt = c
            nxt = max_where(lambda x: x < thr)
            ncnt = count_ge(nxt)
            stay = settled(cnt)
            return jnp.where(stay, thr, nxt), jnp.where(stay, cnt, ncnt)

        thr, cnt = lax.while_loop(unsettled, walk, (thr, cnt))
        thr = jnp.where(searched, thr, NEG_INF)
        has_ties = jnp.max(jnp.where(jnp.logical_and(searched, cnt > kk), 1, 0)) > 0

        @pl.when(jnp.logical_not(has_ties))
        def _():
            write_mask(lambda x, kpos: x >= thr)

        @pl.when(has_ties)
        def _():
            need = kk - count(lambda x, k0: x > thr)
            cut = jnp.zeros((1, cq), jnp.int32)
            bit = seq
            while bit >= 1:
                cand = cut + bit
                before = count(lambda x, k0: jnp.logical_and(x == thr, key_pos(k0, sel_tile) < cand))
                cut = jnp.where(before <= need, cand, cut)
                bit //= 2
            cut = jnp.where(searched, cut, 2 * seq)
            write_mask(lambda x, kpos: jnp.logical_or(x > thr, jnp.logical_and(x == thr, kpos < cut)))

    acc_ref[...] = jnp.zeros(acc_ref.shape, jnp.float32)

    def score_dots(t, slot):
        k0 = pl.multiple_of(t * tk, tk)
        for h in range(C_HEADS):
            s_ref[slot, h] = lax.dot_general(ck_ref[pl.ds(k0, tk), (h // 2) * LANES:(h // 2 + 1) * LANES],
                                             qz_ref[h], _NT, preferred_element_type=jnp.float32)

    def attend(t, slot, stats):
        mask = sc_ref[pl.ds(pl.multiple_of(t * tk, tk), tk), :]
        new_stats, weights = [], []
        for h in range(C_HEADS):
            m_old, l_old = stats[h]
            s = s_ref[slot, h] + mask
            m_new = jnp.maximum(m_old, jnp.max(fold(s, jnp.max), axis=0, keepdims=True))
            alpha = jnp.exp2(m_old - m_new)
            p = jnp.exp2(s - m_new)
            new_stats.append((m_new, alpha * l_old + jnp.sum(fold(p), axis=0, keepdims=True)))
            weights.append((alpha, p.astype(_MXU_DTYPE)))
        for h in range(C_HEADS):
            alpha, p = weights[h]
            acc_ref[h] = alpha * acc_ref[h] + jnp.dot(cvt_ref[t, h * HEAD_DIM:(h + 1) * HEAD_DIM, :], p,
                                                      preferred_element_type=jnp.float32)
        return tuple(new_stats)

    stats = tuple((jnp.full((1, cq), NEG_INF, jnp.float32), jnp.zeros((1, cq), jnp.float32))
                  for _ in range(C_HEADS))
    stats = _two_slot_pipeline(n_tiles, seq // tk - 1, score_dots, attend, stats)
    out = jnp.concatenate([acc_ref[h] * (1.0 / stats[h][1]) for h in range(C_HEADS)], axis=0)
    o_ref[...] = out.T.astype(o_ref.dtype)


def _dsa_attention(ik2, iq, iwt, ck, cvt, cq, bsz, seq, key_tile):
    m = iq.shape[0]
    cqb = C_Q_BLOCK
    nq = seq // cqb
    topk = min(C_TOPK_MAX, seq // 4)
    sel_tile = min(seq, 512)
    assert key_tile == cqb and seq % sel_tile == 0 and sel_tile % key_tile == 0
    return pl.pallas_call(
        functools.partial(_dsa_kernel, topk=topk, key_tile=key_tile, sel_tile=sel_tile, seq=seq),
        grid=(bsz, nq),
        in_specs=[
            pl.BlockSpec((seq, LANES), lambda b, i: (b, 0)),
            pl.BlockSpec((cqb, IDX_HEADS * IDX_DIM), lambda b, i: (b * nq + i, 0)),
            pl.BlockSpec((None, IDX_HEADS, cqb), lambda b, i: (b, 0, i)),
            pl.BlockSpec((seq, C_WIDTH), lambda b, i: (b, 0)),
            pl.BlockSpec((None, seq // key_tile, C_WIDTH, key_tile), lambda b, i: (b, 0, 0, 0)),
            pl.BlockSpec((cqb, C_WIDTH), lambda b, i: (b * nq + i, 0)),
        ],
        out_specs=pl.BlockSpec((cqb, C_WIDTH), lambda b, i: (b * nq + i, 0)),
        out_shape=jax.ShapeDtypeStruct((m, C_WIDTH), _MXU_DTYPE),
        scratch_shapes=[
            pltpu.VMEM((seq, cqb), jnp.float32),
            pltpu.VMEM((IDX_HEADS, cqb, LANES), _MXU_DTYPE),
            pltpu.VMEM((C_HEADS, cqb, LANES), _MXU_DTYPE),
            pltpu.VMEM((C_HEADS, HEAD_DIM, cqb), jnp.float32),
            pltpu.VMEM((2, C_HEADS, key_tile, cqb), jnp.float32),
            pltpu.VMEM((2, IDX_HEADS, key_tile, cqb), jnp.float32),
        ],
        compiler_params=_params(("parallel", "arbitrary")),
        name="dsa_attention",
    )(ik2, iq, iwt, ck, cvt, cq)


def _layer_norm(z, g, b):
    mu = jnp.mean(z, axis=-1, keepdims=True)
    zc = z - mu
    var = jnp.mean(zc * zc, axis=-1, keepdims=True)
    return zc * lax.rsqrt(var + NORM_EPS) * g + b


def _post_kernel(x_ref, oa_ref, ob_ref, oc_ref, p_ref, wo_ref, wup_ref, wdown_ref, wgate_ref, wple_ref,
                 g1_ref, b1_ref, g2_ref, b2_ref, y_ref, *, alpha, ff_tile):
    y = jnp.dot(oa_ref[...], wo_ref[0:A_WIDTH, :], preferred_element_type=jnp.float32)
    y = y + jnp.dot(ob_ref[...], wo_ref[A_WIDTH:A_WIDTH + B_WIDTH, :], preferred_element_type=jnp.float32)
    y = y + jnp.dot(oc_ref[...], wo_ref[A_WIDTH + B_WIDTH:, :], preferred_element_type=jnp.float32)
    x1 = _layer_norm(alpha * x_ref[...] + y, g1_ref[...], b1_ref[...])
    x1b = x1.astype(_MXU_DTYPE)
    ff = jnp.zeros(x1.shape, jnp.float32)
    for f0 in range(0, wup_ref.shape[1], ff_tile):
        u = jnp.dot(x1b, wup_ref[:, f0:f0 + ff_tile], preferred_element_type=jnp.float32)
        u = jnp.square(jnp.maximum(u, 0.0)).astype(_MXU_DTYPE)
        ff = ff + jnp.dot(u, wdown_ref[f0:f0 + ff_tile, :], preferred_element_type=jnp.float32)
    gate = jax.nn.sigmoid(jnp.dot(x1b, wgate_ref[...], preferred_element_type=jnp.float32))
    ple = gate * jnp.dot(p_ref[...].astype(_MXU_DTYPE), wple_ref[...], preferred_element_type=jnp.float32)
    y_ref[...] = _layer_norm(alpha * x1 + ff + ple, g2_ref[...], b2_ref[...])


def _post(x, oa, ob, oc, p, wo, wup, wdown, wgate, wple, g1, b1, g2, b2, alpha, seq):
    m, d = x.shape
    tm = min(seq, 512)

    def rows(width):
        return pl.BlockSpec((tm, width), lambda i: (i, 0))

    weights = [wo, wup, wdown, wgate, wple, g1, b1, g2, b2]
    return pl.pallas_call(
        functools.partial(_post_kernel, alpha=alpha, ff_tile=512),
        grid=(m // tm,),
        in_specs=[rows(d), rows(A_WIDTH), rows(B_WIDTH), rows(C_WIDTH), rows(p.shape[1])]
                 + [_resident(t.shape) for t in weights],
        out_specs=rows(d),
        out_shape=jax.ShapeDtypeStruct((m, d), jnp.float32),
        compiler_params=_params(("parallel",)),
        name="post_mlp",
    )(x, oa, ob, oc, p, *weights)


def kernel(x, p, positions, w_in, rel_bias, lam_q1, lam_k1, lam_q2, lam_k2, diff_norm_g, w_o,
           ln1_g, ln1_b, w_up, w_down, w_ple_gate, w_ple, ln2_g, ln2_b):
    bsz, seq, d_model = x.shape
    depth = w_in.shape[0]
    m = bsz * seq
    assert seq % C_Q_BLOCK == 0 and w_in.shape[2] == IN_TOTAL
    key_tile = C_Q_BLOCK
    alpha = (2 * depth) ** 0.25
    cast = lambda t: t.astype(_MXU_DTYPE)

    cos, sin = _rope_tables(positions)
    xf = x.reshape(m, d_model)
    for i in range(depth):
        w = jnp.pad(cast(w_in[i]), ((0, 0), (0, IN_PADDED - IN_TOTAL)))
        a_qk, avt, bq, bk, bvt, cq, ck, cvt, iq, ik2, iwt = _in_proj(xf, w, cos, sin, bsz, seq, key_tile)
        lam_init = 0.8 - 0.6 * math.exp(-0.3 * i)
        lam_vecs = jnp.stack([lam_q1[i], lam_k1[i], lam_q2[i], lam_k2[i]]).astype(jnp.float32)
        o_a = _band_attention(a_qk, avt, _band_bias(rel_bias[i]), bsz, seq)
        o_b = _diff_attention(bq, bk, bvt, lam_vecs, diff_norm_g[i][None, :].astype(jnp.float32),
                              lam_init, bsz, seq)
        o_c = _dsa_attention(ik2, iq, iwt, ck, cvt, cq, bsz, seq, key_tile)
        row = lambda t: t[i][None, :].astype(jnp.float32)
        xf = _post(xf, o_a, o_b, o_c, p[i].reshape(m, p.shape[-1]),
                   cast(w_o[i]), cast(w_up[i]), cast(w_down[i]), cast(w_ple_gate[i]), cast(w_ple[i]),
                   row(ln1_g), row(ln1_b), row(ln2_g), row(ln2_b), alpha, seq)
    return xf.reshape(bsz, seq, d_model)
```

```python
import functools
import math

import jax
import jax.numpy as jnp
import numpy as np
from jax import lax
from jax.experimental import pallas as pl
from jax.experimental.pallas import tpu as pltpu

CHUNK = 64
HEAD_DIM = 64
A_HEADS = 4
A_LEFT_CHUNKS = 8
REL_CLIP = 128
B_HEADS = 4
B_V_DIM = 2 * HEAD_DIM
C_HEADS = 4
C_TOPK_MAX = 256
IDX_HEADS = 8
IDX_DIM = 64
A_WIDTH = A_HEADS * HEAD_DIM
B_WIDTH = B_HEADS * B_V_DIM
C_WIDTH = C_HEADS * HEAD_DIM
ROPE_THETA = 10000.0
NORM_EPS = 1e-5
NEG_INF = -1e30
LOG2_E = math.log2(math.e)

LANES = 128
FOLD_ROWS = 64
Q_BLOCK = 128
B_Q_BLOCK = 256
B_KEY_TILE = 512
C_Q_BLOCK = 256
VMEM_LIMIT = 56 * 1024 * 1024

OFF_A = 0
OFF_BQ = 3 * A_WIDTH
OFF_BK = OFF_BQ + 2 * B_HEADS * HEAD_DIM
OFF_BV = OFF_BK + 2 * B_HEADS * HEAD_DIM
OFF_CQ = OFF_BV + B_WIDTH
OFF_CK = OFF_CQ + C_WIDTH
OFF_CV = OFF_CK + C_WIDTH
OFF_IQ = OFF_CV + C_WIDTH
OFF_IK = OFF_IQ + IDX_HEADS * IDX_DIM
IN_TOTAL = OFF_IK + IDX_DIM + IDX_HEADS
IN_PADDED = OFF_IK + LANES

_MXU_DTYPE = jnp.bfloat16
_NT = (((1,), (1,)), ((), ()))


def _params(sem, **flags):
    return pltpu.CompilerParams(dimension_semantics=sem, vmem_limit_bytes=VMEM_LIMIT, flags=flags or None)


def _resident(shape):
    return pl.BlockSpec(shape, lambda *_: (0,) * len(shape), pipeline_mode=pl.Buffered(1))


def _two_slot_pipeline(n_tiles, produce, consume, carry, consume_last=None):
    consume_last = consume_last or consume
    n_pairs = (n_tiles - 1) // 2
    produce(0, 0)

    def pair(j, c):
        t0 = 2 * j
        produce(t0 + 1, 1)
        c = consume(t0, 0, c)
        produce(t0 + 2, 0)
        return consume(t0 + 1, 1, c)

    carry = lax.fori_loop(0, n_pairs, pair, carry)
    t0 = 2 * n_pairs

    def one(c):
        return consume_last(t0, 0, c)

    def two(c):
        produce(t0 + 1, 1)
        return consume_last(t0 + 1, 1, consume(t0, 0, c))

    return lax.cond(n_tiles - t0 == 1, one, two, carry)


def _split_heads(t, even):
    zero = jnp.zeros((), t.dtype)
    return jnp.where(even, t, zero), jnp.where(even, zero, t)


def _rope_table_kernel(pos_ref, inv_ref, sgn_ref, cos_ref, sin_ref):
    ang = pos_ref[...].astype(jnp.float32) * inv_ref[...]
    cos_ref[...] = jnp.cos(ang)
    sin_ref[...] = jnp.sin(ang) * sgn_ref[...]


def _rope_tables(positions):
    m = positions.size
    tm = min(m, 2048)
    inv = ROPE_THETA ** (-jnp.arange(0, HEAD_DIM, 2, dtype=jnp.float32) / HEAD_DIM)
    inv = jnp.tile(inv, LANES // (HEAD_DIM // 2))[None, :]
    sgn = jnp.tile(jnp.concatenate([-jnp.ones(HEAD_DIM // 2), jnp.ones(HEAD_DIM // 2)]),
                   LANES // HEAD_DIM).astype(jnp.float32)[None, :]
    row = pl.BlockSpec((tm, LANES), lambda i: (i, 0))
    const = pl.BlockSpec((1, LANES), lambda i: (0, 0))
    return pl.pallas_call(
        _rope_table_kernel,
        grid=(m // tm,),
        in_specs=[pl.BlockSpec((tm, 1), lambda i: (i, 0)), const, const],
        out_specs=[row, row],
        out_shape=[jax.ShapeDtypeStruct((m, LANES), jnp.float32)] * 2,
        compiler_params=_params(("parallel",)),
        name="rope_tables",
    )(positions.reshape(m, 1), inv, sgn)


def _in_proj_kernel(x_ref, w_ref, cos_ref, sin_ref,
                    a_ref, avt_ref, bq_ref, bk_ref, bvt_ref, cq_ref, ck_ref, cvt_ref,
                    iq_ref, ik2_ref, iwt_ref, *, key_tile):
    tm = x_ref.shape[0]
    xb = x_ref[...].astype(_MXU_DTYPE)
    cos = cos_ref[...]
    sin = sin_ref[...]
    lane = lax.broadcasted_iota(jnp.int32, (tm, LANES), 1)
    low_half = (lane & (HEAD_DIM - 1)) < HEAD_DIM // 2
    qk_scale = HEAD_DIM ** -0.5 * LOG2_E

    def proj(c0, n):
        return jnp.dot(xb, w_ref[:, c0:c0 + n], preferred_element_type=jnp.float32)

    def rope(t):
        rot = jnp.where(low_half, pltpu.roll(t, LANES - HEAD_DIM // 2, 1),
                        pltpu.roll(t, HEAD_DIM // 2, 1))
        return t * cos + rot * sin

    def store(ref, h, roped=False, scale=None, col0=0):
        for j in range(h.shape[1] // LANES):
            t = h[:, j * LANES:(j + 1) * LANES]
            if roped:
                t = rope(t)
            if scale is not None:
                t = t * scale
            ref[:, col0 + j * LANES:col0 + (j + 1) * LANES] = t.astype(ref.dtype)

    store(a_ref, proj(OFF_A, A_WIDTH), scale=qk_scale)
    store(a_ref, proj(OFF_A + A_WIDTH, A_WIDTH), col0=A_WIDTH)
    av = proj(OFF_A + 2 * A_WIDTH, A_WIDTH)
    for c in range(tm // Q_BLOCK):
        avt_ref[c] = av[c * Q_BLOCK:(c + 1) * Q_BLOCK, :].T.astype(avt_ref.dtype)
    store(bq_ref, proj(OFF_BQ, OFF_BK - OFF_BQ), roped=True, scale=qk_scale)
    store(bk_ref, proj(OFF_BK, OFF_BV - OFF_BK), roped=True)
    bv = proj(OFF_BV, B_WIDTH)
    b_tile = bvt_ref.shape[-1]
    for c in range(tm // b_tile):
        bvt_ref[c] = bv[c * b_tile:(c + 1) * b_tile, :].T.astype(bvt_ref.dtype)
    store(cq_ref, proj(OFF_CQ, C_WIDTH), roped=True, scale=qk_scale)
    store(ck_ref, proj(OFF_CK, C_WIDTH), roped=True)
    cv = proj(OFF_CV, C_WIDTH)
    for c in range(tm // key_tile):
        cvt_ref[c] = cv[c * key_tile:(c + 1) * key_tile, :].T.astype(cvt_ref.dtype)
    store(iq_ref, proj(OFF_IQ, IDX_HEADS * IDX_DIM), roped=True)
    last = proj(OFF_IK, LANES)
    ikr = rope(last)
    ik2 = jnp.where(lane < IDX_DIM, ikr, pltpu.roll(ikr, IDX_DIM, 1))
    ik2_ref[...] = ik2.astype(ik2_ref.dtype)
    iwt_ref[...] = last.T[IDX_DIM:IDX_DIM + IDX_HEADS, :] * ((IDX_HEADS * IDX_DIM) ** -0.5)


def _in_proj(x, w, cos, sin, bsz, seq, key_tile):
    m, d = x.shape
    tm = min(seq, 1024)
    b_tile = min(seq, B_KEY_TILE)
    n_s = seq // tm
    f = _MXU_DTYPE

    def rows(width):
        return pl.BlockSpec((tm, width), lambda i: (i, 0))

    out_shape = [
        jax.ShapeDtypeStruct((m, 2 * A_WIDTH), f),
        jax.ShapeDtypeStruct((bsz, seq // Q_BLOCK, A_WIDTH, Q_BLOCK), f),
        jax.ShapeDtypeStruct((m, OFF_BK - OFF_BQ), f),
        jax.ShapeDtypeStruct((m, OFF_BV - OFF_BK), f),
        jax.ShapeDtypeStruct((bsz, seq // b_tile, B_WIDTH, b_tile), f),
        jax.ShapeDtypeStruct((m, C_WIDTH), f),
        jax.ShapeDtypeStruct((m, C_WIDTH), f),
        jax.ShapeDtypeStruct((bsz, seq // key_tile, C_WIDTH, key_tile), f),
        jax.ShapeDtypeStruct((m, IDX_HEADS * IDX_DIM), f),
        jax.ShapeDtypeStruct((m, LANES), f),
        jax.ShapeDtypeStruct((bsz, IDX_HEADS, seq), jnp.float32),
    ]
    out_specs = [
        rows(2 * A_WIDTH),
        pl.BlockSpec((None, tm // Q_BLOCK, A_WIDTH, Q_BLOCK), lambda i: (i // n_s, i % n_s, 0, 0)),
        rows(OFF_BK - OFF_BQ), rows(OFF_BV - OFF_BK),
        pl.BlockSpec((None, tm // b_tile, B_WIDTH, b_tile), lambda i: (i // n_s, i % n_s, 0, 0)),
        rows(C_WIDTH), rows(C_WIDTH),
        pl.BlockSpec((None, tm // key_tile, C_WIDTH, key_tile), lambda i: (i // n_s, i % n_s, 0, 0)),
        rows(IDX_HEADS * IDX_DIM), rows(LANES),
        pl.BlockSpec((None, IDX_HEADS, tm), lambda i: (i // n_s, 0, i % n_s)),
    ]
    return pl.pallas_call(
        functools.partial(_in_proj_kernel, key_tile=key_tile),
        grid=(m // tm,),
        in_specs=[rows(d), _resident(w.shape), rows(LANES), rows(LANES)],
        out_specs=out_specs,
        out_shape=out_shape,
        compiler_params=_params(("parallel",)),
        name="in_proj",
    )(x, w, cos, sin)


A_WINDOW_TILES = (A_LEFT_CHUNKS * CHUNK) // Q_BLOCK + 1
A_BLOCKS_PER_STEP = 2


def _band_bias(rel_bias):
    width = A_WINDOW_TILES * Q_BLOCK
    span = Q_BLOCK + width - 1
    d = np.arange(span) - (width - 1) + A_LEFT_CHUNKS * CHUNK
    gen = rel_bias[:, np.clip(d, -REL_CLIP, REL_CLIP) + REL_CLIP].astype(jnp.float32)
    sheared = jnp.tile(gen, (1, Q_BLOCK + 1))[:, :Q_BLOCK * (span + 1)].reshape(-1, Q_BLOCK, span + 1)
    bias = sheared[:, :, :width][:, :, ::-1]
    r = np.arange(Q_BLOCK)[:, None]
    e = np.arange(width)[None, :]
    back = r // CHUNK + A_LEFT_CHUNKS - e // CHUNK
    in_band = (back >= 0) & (back <= A_LEFT_CHUNKS)
    bias = jnp.where(jnp.asarray(in_band)[None], bias * LOG2_E, NEG_INF)
    return bias.reshape(A_HEADS // 2, 2, Q_BLOCK, width).transpose(0, 3, 1, 2).reshape(A_HEADS // 2, width, 2 * Q_BLOCK)


def _band_kernel(q_ref, k_ref, vt_ref, bias_ref, o_ref):
    even_head = lax.broadcasted_iota(jnp.int32, (Q_BLOCK, LANES), 1) < HEAD_DIM
    problems = []
    for sub in range(A_BLOCKS_PER_STEP):
        i = pl.program_id(1) * A_BLOCKS_PER_STEP + sub
        rows = slice(sub * Q_BLOCK, (sub + 1) * Q_BLOCK)
        tiles = []
        for tt in range(A_WINDOW_TILES):
            t = i - (A_WINDOW_TILES - 1) + tt
            tiles.append((t, jnp.maximum(t, 0)))
        for pair in range(A_HEADS // 2):
            problems.append((rows, slice(pair * LANES, (pair + 1) * LANES), pair, tiles))
    scores = []
    for rows, cols, pair, tiles in problems:
        qs = jnp.concatenate(_split_heads(q_ref[rows, cols], even_head), axis=0)
        scores.append([lax.dot_general(k_ref[pl.ds(pl.multiple_of(tc * Q_BLOCK, Q_BLOCK), Q_BLOCK), cols],
                                       qs, _NT, preferred_element_type=jnp.float32)
                       for t, tc in tiles])
    probs = []
    for (rows, cols, pair, tiles), s_raw in zip(problems, scores):
        s_tiles = [jnp.where(t >= 0, s + bias_ref[pair, tt * Q_BLOCK:(tt + 1) * Q_BLOCK, :], NEG_INF)
                   for tt, ((t, tc), s) in enumerate(zip(tiles, s_raw))]
        m = jnp.max(functools.reduce(jnp.maximum, s_tiles), axis=0, keepdims=True)
        p_tiles = [jnp.exp2(s - m) for s in s_tiles]
        l = jnp.sum(functools.reduce(jnp.add, p_tiles), axis=0, keepdims=True)
        probs.append(([p.astype(_MXU_DTYPE) for p in p_tiles], l))
    for (rows, cols, pair, tiles), (p_tiles, l) in zip(problems, probs):
        acc = jnp.zeros((LANES, 2 * Q_BLOCK), jnp.float32)
        for (t, tc), p in zip(tiles, p_tiles):
            acc = acc + jnp.dot(vt_ref[tc, cols, :], p, preferred_element_type=jnp.float32)
        acc = acc * (1.0 / l)
        out_t = jnp.concatenate([acc[:HEAD_DIM, :Q_BLOCK], acc[HEAD_DIM:, Q_BLOCK:]], axis=0)
        o_ref[rows, cols] = out_t.T.astype(o_ref.dtype)


def _band_attention(a_qk, avt, bias, bsz, seq):
    m = a_qk.shape[0]
    step_rows = A_BLOCKS_PER_STEP * Q_BLOCK
    nq = seq // step_rows
    return pl.pallas_call(
        _band_kernel,
        grid=(bsz, nq),
        in_specs=[
            pl.BlockSpec((step_rows, A_WIDTH), lambda b, i: (b * nq + i, 0)),
            pl.BlockSpec((seq, A_WIDTH), lambda b, i: (b, 1)),
            pl.BlockSpec((None, seq // Q_BLOCK, A_WIDTH, Q_BLOCK), lambda b, i: (b, 0, 0, 0)),
            _resident(bias.shape),
        ],
        out_specs=pl.BlockSpec((step_rows, A_WIDTH), lambda b, i: (b * nq + i, 0)),
        out_shape=jax.ShapeDtypeStruct((m, A_WIDTH), _MXU_DTYPE),
        compiler_params=_params(("parallel", "arbitrary")),
        name="band_attention",
    )(a_qk, a_qk, avt, bias)


def _diff_kernel(lam_ref, g_ref, q_ref, k_ref, vt_ref, o_ref, qs_ref, acc_ref, s_ref, p_ref,
                 *, lam_init, key_tile):
    i = pl.program_id(1)
    tk = key_tile
    bq = B_Q_BLOCK
    nq2 = 2 * bq
    lv = lam_ref[...]
    lam = (jnp.exp(jnp.sum(lv[0:1] * lv[1:2], axis=1, keepdims=True))
           - jnp.exp(jnp.sum(lv[2:3] * lv[3:4], axis=1, keepdims=True)) + lam_init)
    first_map = lax.broadcasted_iota(jnp.int32, (bq, LANES), 1) < HEAD_DIM
    for h in range(B_HEADS):
        qs_ref[h] = jnp.concatenate(_split_heads(q_ref[:, h * LANES:(h + 1) * LANES], first_map), axis=0)
    acc_ref[...] = jnp.zeros(acc_ref.shape, jnp.float32)
    col = lax.broadcasted_iota(jnp.int32, (1, nq2), 1)
    limit = i * bq + ((col & (bq - 1)) // CHUNK + 1) * CHUNK
    n_tiles = ((i + 1) * bq + tk - 1) // tk

    def fold(x, op):
        return op(x.reshape(tk // FOLD_ROWS, FOLD_ROWS, nq2), axis=0)

    def score_dots(t):
        k0 = pl.multiple_of(t * tk, tk)
        for h in range(B_HEADS):
            s_ref[h] = lax.dot_general(k_ref[pl.ds(k0, tk), h * LANES:(h + 1) * LANES], qs_ref[h], _NT,
                                       preferred_element_type=jnp.float32)

    def softmax_update(t, stats, masked):
        new_stats, weights = [], []
        for h in range(B_HEADS):
            s = s_ref[h]
            if masked:
                s = jnp.where(t * tk + lax.broadcasted_iota(jnp.int32, s.shape, 0) < limit, s, NEG_INF)
            m_old, l_old = stats[h]
            m_new = jnp.maximum(m_old, jnp.max(fold(s, jnp.max), axis=0, keepdims=True))
            alpha = jnp.exp2(m_old - m_new)
            p = jnp.exp2(s - m_new)
            new_stats.append((m_new, alpha * l_old + jnp.sum(fold(p, jnp.sum), axis=0, keepdims=True)))
            p_ref[h] = p.astype(_MXU_DTYPE)
            weights.append(alpha)
        return tuple(new_stats), weights

    def value_dots(t, weights):
        for h in range(B_HEADS):
            acc_ref[h] = weights[h] * acc_ref[h] + jnp.dot(vt_ref[t, h * B_V_DIM:(h + 1) * B_V_DIM, :], p_ref[h],
                                                      preferred_element_type=jnp.float32)

    def step(t, stats):
        stats, weights = softmax_update(t, stats, masked=False)
        score_dots(t + 1)
        value_dots(t, weights)
        return stats

    stats = tuple((jnp.full((1, nq2), NEG_INF, jnp.float32), jnp.zeros((1, nq2), jnp.float32))
                  for _ in range(B_HEADS))
    score_dots(0)
    stats = lax.fori_loop(0, n_tiles - 1, step, stats)
    stats, weights = softmax_update(n_tiles - 1, stats, masked=True)
    value_dots(n_tiles - 1, weights)

    for h in range(B_HEADS):
        acc = acc_ref[h] * (1.0 / stats[h][1])
        o = (acc[:, :bq] - acc[:, bq:] * lam).T
        ms = jnp.mean(o * o, axis=1, keepdims=True)
        o = o * lax.rsqrt(ms + NORM_EPS) * g_ref[...] * (1.0 - lam_init)
        o_ref[:, h * B_V_DIM:(h + 1) * B_V_DIM] = o.astype(o_ref.dtype)


def _diff_attention(bq, bk, bvt, lam_vecs, gain, lam_init, bsz, seq):
    m = bq.shape[0]
    nq = seq // B_Q_BLOCK
    n_kt, key_tile = bvt.shape[1], bvt.shape[3]
    return pl.pallas_call(
        functools.partial(_diff_kernel, lam_init=lam_init, key_tile=key_tile),
        grid=(bsz, nq),
        in_specs=[
            _resident(lam_vecs.shape),
            _resident(gain.shape),
            pl.BlockSpec((B_Q_BLOCK, B_HEADS * LANES), lambda b, i: (b * nq + i, 0)),
            pl.BlockSpec((seq, B_HEADS * LANES), lambda b, i: (b, 0)),
            pl.BlockSpec((None, n_kt, B_WIDTH, key_tile), lambda b, i: (b, 0, 0, 0)),
        ],
        out_specs=pl.BlockSpec((B_Q_BLOCK, B_WIDTH), lambda b, i: (b * nq + i, 0)),
        out_shape=jax.ShapeDtypeStruct((m, B_WIDTH), _MXU_DTYPE),
        scratch_shapes=[
            pltpu.VMEM((B_HEADS, 2 * B_Q_BLOCK, LANES), _MXU_DTYPE),
            pltpu.VMEM((B_HEADS, B_V_DIM, 2 * B_Q_BLOCK), jnp.float32),
            pltpu.VMEM((B_HEADS, key_tile, 2 * B_Q_BLOCK), jnp.float32),
            pltpu.VMEM((B_HEADS, key_tile, 2 * B_Q_BLOCK), _MXU_DTYPE),
        ],
        compiler_params=_params(("parallel", "arbitrary")),
        name="diff_attention",
    )(lam_vecs, gain, bq, bk, bvt)


C_BISECT_STEPS = 14


def _dsa_kernel(ik_ref, iq_ref, iw_ref, ck_ref, cvt_ref, cq_ref, o_ref,
                sc_ref, iqz_ref, qz_ref, acc_ref, s_ref, lg_ref, *, topk, key_tile, sel_tile, seq):
    i = pl.program_id(1)
    tk = key_tile
    cq = C_Q_BLOCK
    n_keys = (i + 1) * cq
    n_tiles = (n_keys + tk - 1) // tk
    n_sel = (n_keys + sel_tile - 1) // sel_tile
    lane_q = lax.broadcasted_iota(jnp.int32, (1, cq), 1)
    limit = i * cq + (lane_q // CHUNK + 1) * CHUNK
    searched = limit > topk
    even_head = lax.broadcasted_iota(jnp.int32, (cq, LANES), 1) < HEAD_DIM
    kk = float(topk)

    def fold(x, op=jnp.sum):
        return op(x.reshape(x.shape[0] // FOLD_ROWS, FOLD_ROWS, cq), axis=0)

    def key_pos(k0, rows):
        return k0 + lax.broadcasted_iota(jnp.int32, (rows, cq), 0)

    for j in range(IDX_HEADS // 2):
        iqz_ref[2 * j], iqz_ref[2 * j + 1] = _split_heads(iq_ref[:, j * LANES:(j + 1) * LANES], even_head)
    for j in range(C_HEADS // 2):
        qz_ref[2 * j], qz_ref[2 * j + 1] = _split_heads(cq_ref[:, j * LANES:(j + 1) * LANES], even_head)
    w = iw_ref[...]

    def index_dots(t, slot):
        ikt = ik_ref[pl.ds(pl.multiple_of(t * tk, tk), tk), :]
        for h in range(IDX_HEADS):
            lg_ref[slot, h] = lax.dot_general(ikt, iqz_ref[h], _NT, preferred_element_type=jnp.float32)

    def index_combine(t, slot, c):
        k0 = pl.multiple_of(t * tk, tk)
        acc = jnp.zeros((tk, cq), jnp.float32)
        for h in range(IDX_HEADS):
            acc = acc + jnp.maximum(lg_ref[slot, h], 0.0) * w[h:h + 1, :]
        adm = key_pos(k0, tk) < limit
        sc_ref[pl.ds(k0, tk), :] = jnp.where(adm, acc, NEG_INF)
        return (jnp.minimum(c[0], fold(jnp.where(adm, acc, jnp.inf), jnp.min)),
                jnp.maximum(c[1], fold(jnp.where(adm, acc, NEG_INF), jnp.max)))

    lo8, hi8 = _two_slot_pipeline(n_tiles, index_dots, index_combine,
                                  (jnp.full((FOLD_ROWS, cq), jnp.inf, jnp.float32),
                                   jnp.full((FOLD_ROWS, cq), NEG_INF, jnp.float32)))

    @pl.when(n_tiles * tk < n_sel * sel_tile)
    def _():
        k0 = pl.multiple_of(n_tiles * tk, tk)
        sc_ref[pl.ds(k0, tk), :] = jnp.full((tk, cq), NEG_INF, jnp.float32)

    def over_scores(fn, init):
        def body(t, acc):
            k0 = pl.multiple_of(t * sel_tile, sel_tile)
            return fn(sc_ref[pl.ds(k0, sel_tile), :], k0, acc)
        return lax.fori_loop(0, n_sel, body, init)

    def count(pred):
        def body(x, k0, acc):
            hit = pred(x, k0)
            for g in range(sel_tile // FOLD_ROWS):
                acc = jnp.where(hit[g * FOLD_ROWS:(g + 1) * FOLD_ROWS], acc + 1.0, acc)
            return acc
        part = over_scores(body, jnp.zeros((FOLD_ROWS, cq), jnp.float32))
        return jnp.sum(part, axis=0, keepdims=True)

    def count_ge(thr):
        return count(lambda x, k0: x >= thr)

    def max_where(pred):
        part = over_scores(lambda x, k0, acc: jnp.maximum(acc, fold(jnp.where(pred(x), x, NEG_INF), jnp.max)),
                           jnp.full((FOLD_ROWS, cq), NEG_INF, jnp.float32))
        return jnp.max(part, axis=0, keepdims=True)

    def write_mask(keep_fn):
        def body(t, carry):
            k0 = pl.multiple_of(t * tk, tk)
            x = sc_ref[pl.ds(k0, tk), :]
            kpos = key_pos(k0, tk)
            keep = jnp.logical_and(keep_fn(x, kpos), kpos < limit)
            sc_ref[pl.ds(k0, tk), :] = jnp.where(keep, 0.0, NEG_INF)
            return carry
        lax.fori_loop(0, n_tiles, body, 0)

    @pl.when(n_keys <= topk)
    def _():
        write_mask(lambda x, kpos: kpos >= 0)

    @pl.when(n_keys > topk)
    def _():
        lo = jnp.min(lo8, axis=0, keepdims=True)
        hi = jnp.max(hi8, axis=0, keepdims=True)

        def bisect(_, c):
            lo, hi = c
            mid = lo + (hi - lo) * 0.5
            ok = count_ge(mid) >= kk
            return jnp.where(ok, mid, lo), jnp.where(ok, hi, mid)

        lo, hi = lax.fori_loop(0, C_BISECT_STEPS, bisect, (lo, hi))

        def settled(cnt):
            return jnp.logical_or(cnt >= kk, jnp.logical_not(searched))

        thr = max_where(lambda x: x <= hi)
        cnt = count_ge(thr)

        def unsettled(c):
            return jnp.max(jnp.where(settled(c[1]), 0, 1)) > 0

        def walk(c):
            thr, cnt = c
            nxt = max_where(lambda x: x < thr)
            ncnt = count_ge(nxt)
            stay = settled(cnt)
            return jnp.where(stay, thr, nxt), jnp.where(stay, cnt, ncnt)

        thr, cnt = lax.while_loop(unsettled, walk, (thr, cnt))
        thr = jnp.where(searched, thr, NEG_INF)
        has_ties = jnp.max(jnp.where(jnp.logical_and(searched, cnt > kk), 1, 0)) > 0

        @pl.when(jnp.logical_not(has_ties))
        def _():
            write_mask(lambda x, kpos: x >= thr)

        @pl.when(has_ties)
        def _():
            need = kk - count(lambda x, k0: x > thr)
            cut = jnp.zeros((1, cq), jnp.int32)
            bit = seq
            while bit >= 1:
                cand = cut + bit
                before = count(lambda x, k0: jnp.logical_and(x == thr, key_pos(k0, sel_tile) < cand))
                cut = jnp.where(before <= need, cand, cut)
                bit //= 2
            cut = jnp.where(searched, cut, 2 * seq)
            write_mask(lambda x, kpos: jnp.logical_or(x > thr, jnp.logical_and(x == thr, kpos < cut)))

    acc_ref[...] = jnp.zeros(acc_ref.shape, jnp.float32)

    def score_dots(t, slot):
        k0 = pl.multiple_of(t * tk, tk)
        for h in range(C_HEADS):
            s_ref[slot, h] = lax.dot_general(ck_ref[pl.ds(k0, tk), (h // 2) * LANES:(h // 2 + 1) * LANES],
                                             qz_ref[h], _NT, preferred_element_type=jnp.float32)

    def attend(t, slot, stats):
        mask = sc_ref[pl.ds(pl.multiple_of(t * tk, tk), tk), :]
        new_stats, weights = [], []
        for h in range(C_HEADS):
            m_old, l_old = stats[h]
            s = s_ref[slot, h] + mask
            m_new = jnp.maximum(m_old, jnp.max(fold(s, jnp.max), axis=0, keepdims=True))
            alpha = jnp.exp2(m_old - m_new)
            p = jnp.exp2(s - m_new)
            new_stats.append((m_new, alpha * l_old + jnp.sum(fold(p), axis=0, keepdims=True)))
            weights.append((alpha, p.astype(_MXU_DTYPE)))
        for h in range(C_HEADS):
            alpha, p = weights[h]
            acc_ref[h] = alpha * acc_ref[h] + jnp.dot(cvt_ref[t, h * HEAD_DIM:(h + 1) * HEAD_DIM, :], p,
                                                      preferred_element_type=jnp.float32)
        return tuple(new_stats)

    stats = tuple((jnp.full((1, cq), NEG_INF, jnp.float32), jnp.zeros((1, cq), jnp.float32))
                  for _ in range(C_HEADS))
    stats = _two_slot_pipeline(n_tiles, score_dots, attend, stats)
    out = jnp.concatenate([acc_ref[h] * (1.0 / stats[h][1]) for h in range(C_HEADS)], axis=0)
    o_ref[...] = out.T.astype(o_ref.dtype)


def _dsa_attention(ik2, iq, iwt, ck, cvt, cq, bsz, seq, key_tile):
    m = iq.shape[0]
    cqb = C_Q_BLOCK
    nq = seq // cqb
    topk = min(C_TOPK_MAX, seq // 4)
    sel_tile = min(seq, 512)
    assert key_tile == cqb and seq % sel_tile == 0 and sel_tile % key_tile == 0
    return pl.pallas_call(
        functools.partial(_dsa_kernel, topk=topk, key_tile=key_tile, sel_tile=sel_tile, seq=seq),
        grid=(bsz, nq),
        in_specs=[
            pl.BlockSpec((seq, LANES), lambda b, i: (b, 0)),
            pl.BlockSpec((cqb, IDX_HEADS * IDX_DIM), lambda b, i: (b * nq + i, 0)),
            pl.BlockSpec((None, IDX_HEADS, cqb), lambda b, i: (b, 0, i)),
            pl.BlockSpec((seq, C_WIDTH), lambda b, i: (b, 0)),
            pl.BlockSpec((None, seq // key_tile, C_WIDTH, key_tile), lambda b, i: (b, 0, 0, 0)),
            pl.BlockSpec((cqb, C_WIDTH), lambda b, i: (b * nq + i, 0)),
        ],
        out_specs=pl.BlockSpec((cqb, C_WIDTH), lambda b, i: (b * nq + i, 0)),
        out_shape=jax.ShapeDtypeStruct((m, C_WIDTH), _MXU_DTYPE),
        scratch_shapes=[
            pltpu.VMEM((seq, cqb), jnp.float32),
            pltpu.VMEM((IDX_HEADS, cqb, LANES), _MXU_DTYPE),
            pltpu.VMEM((C_HEADS, cqb, LANES), _MXU_DTYPE),
            pltpu.VMEM((C_HEADS, HEAD_DIM, cqb), jnp.float32),
            pltpu.VMEM((2, C_HEADS, key_tile, cqb), jnp.float32),
            pltpu.VMEM((2, IDX_HEADS, key_tile, cqb), jnp.float32),
        ],
        compiler_params=_params(("parallel", "arbitrary")),
        name="dsa_attention",
    )(ik2, iq, iwt, ck, cvt, cq)


def _layer_norm(z, g, b):
    mu = jnp.mean(z, axis=-1, keepdims=True)
    zc = z - mu
    var = jnp.mean(zc * zc, axis=-1, keepdims=True)
    return zc * lax.rsqrt(var + NORM_EPS) * g + b


def _post_kernel(x_ref, oa_ref, ob_ref, oc_ref, p_ref, wo_ref, wup_ref, wdown_ref, wgate_ref, wple_ref,
                 g1_ref, b1_ref, g2_ref, b2_ref, y_ref, *, alpha, ff_tile):
    y = jnp.dot(oa_ref[...], wo_ref[0:A_WIDTH, :], preferred_element_type=jnp.float32)
    y = y + jnp.dot(ob_ref[...], wo_ref[A_WIDTH:A_WIDTH + B_WIDTH, :], preferred_element_type=jnp.float32)
    y = y + jnp.dot(oc_ref[...], wo_ref[A_WIDTH + B_WIDTH:, :], preferred_element_type=jnp.float32)
    emb = jnp.dot(p_ref[...].astype(_MXU_DTYPE), wple_ref[...], preferred_element_type=jnp.float32)
    x1 = _layer_norm(alpha * x_ref[...] + y, g1_ref[...], b1_ref[...])
    x1b = x1.astype(_MXU_DTYPE)
    ff = jnp.zeros(x1.shape, jnp.float32)
    for f0 in range(0, wup_ref.shape[1], ff_tile):
        u = jnp.dot(x1b, wup_ref[:, f0:f0 + ff_tile], preferred_element_type=jnp.float32)
        u = jnp.square(jnp.maximum(u, 0.0)).astype(_MXU_DTYPE)
        ff = ff + jnp.dot(u, wdown_ref[f0:f0 + ff_tile, :], preferred_element_type=jnp.float32)
    gate = jax.nn.sigmoid(jnp.dot(x1b, wgate_ref[...], preferred_element_type=jnp.float32))
    y_ref[...] = _layer_norm(alpha * x1 + ff + gate * emb, g2_ref[...], b2_ref[...])


def _post(x, oa, ob, oc, p_all, layer, wo, wup, wdown, wgate, wple, g1, b1, g2, b2, alpha, seq):
    m, d = x.shape
    tm = min(seq, 512)
    steps = m // tm

    def rows(width):
        return pl.BlockSpec((tm, width), lambda i: (i, 0))

    weights = [wo, wup, wdown, wgate, wple, g1, b1, g2, b2]
    return pl.pallas_call(
        functools.partial(_post_kernel, alpha=alpha, ff_tile=512),
        grid=(steps,),
        in_specs=[rows(d), rows(A_WIDTH), rows(B_WIDTH), rows(C_WIDTH),
                  pl.BlockSpec((tm, p_all.shape[1]), lambda i: (layer * steps + i, 0))]
                 + [_resident(t.shape) for t in weights],
        out_specs=rows(d),
        out_shape=jax.ShapeDtypeStruct((m, d), jnp.float32),
        compiler_params=_params(("parallel",)),
        name="post_mlp",
    )(x, oa, ob, oc, p_all, *weights)


def kernel(x, p, positions, w_in, rel_bias, lam_q1, lam_k1, lam_q2, lam_k2, diff_norm_g, w_o,
           ln1_g, ln1_b, w_up, w_down, w_ple_gate, w_ple, ln2_g, ln2_b):
    bsz, seq, d_model = x.shape
    depth = w_in.shape[0]
    m = bsz * seq
    assert seq % C_Q_BLOCK == 0 and w_in.shape[2] == IN_TOTAL
    key_tile = C_Q_BLOCK
    alpha = (2 * depth) ** 0.25
    cast = lambda t: t.astype(_MXU_DTYPE)

    cos, sin = _rope_tables(positions)
    xf = x.reshape(m, d_model)
    for i in range(depth):
        w = jnp.pad(cast(w_in[i]), ((0, 0), (0, IN_PADDED - IN_TOTAL)))
        a_qk, avt, bq, bk, bvt, cq, ck, cvt, iq, ik2, iwt = _in_proj(xf, w, cos, sin, bsz, seq, key_tile)
        lam_init = 0.8 - 0.6 * math.exp(-0.3 * i)
        lam_vecs = jnp.stack([lam_q1[i], lam_k1[i], lam_q2[i], lam_k2[i]]).astype(jnp.float32)
        o_a = _band_attention(a_qk, avt, _band_bias(rel_bias[i]), bsz, seq)
        o_b = _diff_attention(bq, bk, bvt, lam_vecs, diff_norm_g[i][None, :].astype(jnp.float32),
                              lam_init, bsz, seq)
        o_c = _dsa_attention(ik2, iq, iwt, ck, cvt, cq, bsz, seq, key_tile)
        row = lambda t: t[i][None, :].astype(jnp.float32)
        xf = _post(xf, o_a, o_b, o_c, p.reshape(depth * m, p.shape[-1]), i,
                   cast(w_o[i]), cast(w_up[i]), cast(w_down[i]), cast(w_ple_gate[i]), cast(w_ple[i]),
                   row(ln1_g), row(ln1_b), row(ln2_g), row(ln2_b), alpha, seq)
    return xf.reshape(bsz, seq, d_model)
```

```python
import functools
import math

import jax
import jax.numpy as jnp
import numpy as np
from jax import lax
from jax.experimental import pallas as pl
from jax.experimental.pallas import tpu as pltpu

CHUNK = 64
HEAD_DIM = 64
A_HEADS = 4
A_LEFT_CHUNKS = 8
REL_CLIP = 128
B_HEADS = 4
B_V_DIM = 2 * HEAD_DIM
C_HEADS = 4
C_TOPK_MAX = 256
IDX_HEADS = 8
IDX_DIM = 64
A_WIDTH = A_HEADS * HEAD_DIM
B_WIDTH = B_HEADS * B_V_DIM
C_WIDTH = C_HEADS * HEAD_DIM
ROPE_THETA = 10000.0
NORM_EPS = 1e-5
NEG_INF = -1e30
LOG2_E = math.log2(math.e)

LANES = 128
FOLD_ROWS = 64
Q_BLOCK = 128
B_Q_BLOCK = 512
B_KEY_TILE = 512
C_Q_BLOCK = 256
VMEM_LIMIT = 56 * 1024 * 1024

OFF_A = 0
OFF_BQ = 3 * A_WIDTH
OFF_BK = OFF_BQ + 2 * B_HEADS * HEAD_DIM
OFF_BV = OFF_BK + 2 * B_HEADS * HEAD_DIM
OFF_CQ = OFF_BV + B_WIDTH
OFF_CK = OFF_CQ + C_WIDTH
OFF_CV = OFF_CK + C_WIDTH
OFF_IQ = OFF_CV + C_WIDTH
OFF_IK = OFF_IQ + IDX_HEADS * IDX_DIM
IN_TOTAL = OFF_IK + IDX_DIM + IDX_HEADS
IN_PADDED = OFF_IK + LANES

_MXU_DTYPE = jnp.bfloat16
_NT = (((1,), (1,)), ((), ()))


def _params(sem, **flags):
    return pltpu.CompilerParams(dimension_semantics=sem, vmem_limit_bytes=VMEM_LIMIT, flags=flags or None)


def _resident(shape):
    return pl.BlockSpec(shape, lambda *_: (0,) * len(shape), pipeline_mode=pl.Buffered(1))


def _two_slot_pipeline(n_tiles, produce, consume, carry, consume_last=None):
    consume_last = consume_last or consume
    n_pairs = (n_tiles - 1) // 2
    produce(0, 0)

    def pair(j, c):
        t0 = 2 * j
        produce(t0 + 1, 1)
        c = consume(t0, 0, c)
        produce(t0 + 2, 0)
        return consume(t0 + 1, 1, c)

    carry = lax.fori_loop(0, n_pairs, pair, carry)
    t0 = 2 * n_pairs

    def one(c):
        return consume_last(t0, 0, c)

    def two(c):
        produce(t0 + 1, 1)
        return consume_last(t0 + 1, 1, consume(t0, 0, c))

    return lax.cond(n_tiles - t0 == 1, one, two, carry)


def _split_heads(t, even):
    zero = jnp.zeros((), t.dtype)
    return jnp.where(even, t, zero), jnp.where(even, zero, t)


def _rope_table_kernel(pos_ref, inv_ref, sgn_ref, cos_ref, sin_ref):
    ang = pos_ref[...].astype(jnp.float32) * inv_ref[...]
    cos_ref[...] = jnp.cos(ang)
    sin_ref[...] = jnp.sin(ang) * sgn_ref[...]


def _rope_tables(positions):
    m = positions.size
    tm = min(m, 2048)
    inv = ROPE_THETA ** (-jnp.arange(0, HEAD_DIM, 2, dtype=jnp.float32) / HEAD_DIM)
    inv = jnp.tile(inv, LANES // (HEAD_DIM // 2))[None, :]
    sgn = jnp.tile(jnp.concatenate([-jnp.ones(HEAD_DIM // 2), jnp.ones(HEAD_DIM // 2)]),
                   LANES // HEAD_DIM).astype(jnp.float32)[None, :]
    row = pl.BlockSpec((tm, LANES), lambda i: (i, 0))
    const = pl.BlockSpec((1, LANES), lambda i: (0, 0))
    return pl.pallas_call(
        _rope_table_kernel,
        grid=(m // tm,),
        in_specs=[pl.BlockSpec((tm, 1), lambda i: (i, 0)), const, const],
        out_specs=[row, row],
        out_shape=[jax.ShapeDtypeStruct((m, LANES), jnp.float32)] * 2,
        compiler_params=_params(("parallel",)),
        name="rope_tables",
    )(positions.reshape(m, 1), inv, sgn)


def _in_proj_kernel(x_ref, w_ref, cos_ref, sin_ref,
                    a_ref, avt_ref, bq_ref, bk_ref, bvt_ref, cq_ref, ck_ref, cvt_ref,
                    iq_ref, ik2_ref, iwt_ref, *, key_tile):
    tm = x_ref.shape[0]
    xb = x_ref[...].astype(_MXU_DTYPE)
    cos = cos_ref[...]
    sin = sin_ref[...]
    lane = lax.broadcasted_iota(jnp.int32, (tm, LANES), 1)
    low_half = (lane & (HEAD_DIM - 1)) < HEAD_DIM // 2
    qk_scale = HEAD_DIM ** -0.5 * LOG2_E

    def proj(c0, n):
        return jnp.dot(xb, w_ref[:, c0:c0 + n], preferred_element_type=jnp.float32)

    def rope(t):
        rot = jnp.where(low_half, pltpu.roll(t, LANES - HEAD_DIM // 2, 1),
                        pltpu.roll(t, HEAD_DIM // 2, 1))
        return t * cos + rot * sin

    def store(ref, h, roped=False, scale=None, col0=0):
        for j in range(h.shape[1] // LANES):
            t = h[:, j * LANES:(j + 1) * LANES]
            if roped:
                t = rope(t)
            if scale is not None:
                t = t * scale
            ref[:, col0 + j * LANES:col0 + (j + 1) * LANES] = t.astype(ref.dtype)

    store(a_ref, proj(OFF_A, A_WIDTH), scale=qk_scale)
    store(a_ref, proj(OFF_A + A_WIDTH, A_WIDTH), col0=A_WIDTH)
    av = proj(OFF_A + 2 * A_WIDTH, A_WIDTH)
    for c in range(tm // Q_BLOCK):
        avt_ref[c] = av[c * Q_BLOCK:(c + 1) * Q_BLOCK, :].T.astype(avt_ref.dtype)
    store(bq_ref, proj(OFF_BQ, OFF_BK - OFF_BQ), roped=True, scale=qk_scale)
    store(bk_ref, proj(OFF_BK, OFF_BV - OFF_BK), roped=True)
    bv = proj(OFF_BV, B_WIDTH)
    b_tile = bvt_ref.shape[-1]
    for c in range(tm // b_tile):
        bvt_ref[c] = bv[c * b_tile:(c + 1) * b_tile, :].T.astype(bvt_ref.dtype)
    store(cq_ref, proj(OFF_CQ, C_WIDTH), roped=True, scale=qk_scale)
    store(ck_ref, proj(OFF_CK, C_WIDTH), roped=True)
    cv = proj(OFF_CV, C_WIDTH)
    for c in range(tm // key_tile):
        cvt_ref[c] = cv[c * key_tile:(c + 1) * key_tile, :].T.astype(cvt_ref.dtype)
    store(iq_ref, proj(OFF_IQ, IDX_HEADS * IDX_DIM), roped=True)
    last = proj(OFF_IK, LANES)
    ikr = rope(last)
    ik2 = jnp.where(lane < IDX_DIM, ikr, pltpu.roll(ikr, IDX_DIM, 1))
    ik2_ref[...] = ik2.astype(ik2_ref.dtype)
    iwt_ref[...] = last.T[IDX_DIM:IDX_DIM + IDX_HEADS, :] * ((IDX_HEADS * IDX_DIM) ** -0.5)


def _in_proj(x, w, cos, sin, bsz, seq, key_tile):
    m, d = x.shape
    tm = min(seq, 1024)
    b_tile = min(seq, B_KEY_TILE)
    n_s = seq // tm
    f = _MXU_DTYPE

    def rows(width):
        return pl.BlockSpec((tm, width), lambda i: (i, 0))

    out_shape = [
        jax.ShapeDtypeStruct((m, 2 * A_WIDTH), f),
        jax.ShapeDtypeStruct((bsz, seq // Q_BLOCK, A_WIDTH, Q_BLOCK), f),
        jax.ShapeDtypeStruct((m, OFF_BK - OFF_BQ), f),
        jax.ShapeDtypeStruct((m, OFF_BV - OFF_BK), f),
        jax.ShapeDtypeStruct((bsz, seq // b_tile, B_WIDTH, b_tile), f),
        jax.ShapeDtypeStruct((m, C_WIDTH), f),
        jax.ShapeDtypeStruct((m, C_WIDTH), f),
        jax.ShapeDtypeStruct((bsz, seq // key_tile, C_WIDTH, key_tile), f),
        jax.ShapeDtypeStruct((m, IDX_HEADS * IDX_DIM), f),
        jax.ShapeDtypeStruct((m, LANES), f),
        jax.ShapeDtypeStruct((bsz, IDX_HEADS, seq), jnp.float32),
    ]
    out_specs = [
        rows(2 * A_WIDTH),
        pl.BlockSpec((None, tm // Q_BLOCK, A_WIDTH, Q_BLOCK), lambda i: (i // n_s, i % n_s, 0, 0)),
        rows(OFF_BK - OFF_BQ), rows(OFF_BV - OFF_BK),
        pl.BlockSpec((None, tm // b_tile, B_WIDTH, b_tile), lambda i: (i // n_s, i % n_s, 0, 0)),
        rows(C_WIDTH), rows(C_WIDTH),
        pl.BlockSpec((None, tm // key_tile, C_WIDTH, key_tile), lambda i: (i // n_s, i % n_s, 0, 0)),
        rows(IDX_HEADS * IDX_DIM), rows(LANES),
        pl.BlockSpec((None, IDX_HEADS, tm), lambda i: (i // n_s, 0, i % n_s)),
    ]
    return pl.pallas_call(
        functools.partial(_in_proj_kernel, key_tile=key_tile),
        grid=(m // tm,),
        in_specs=[rows(d), _resident(w.shape), rows(LANES), rows(LANES)],
        out_specs=out_specs,
        out_shape=out_shape,
        compiler_params=_params(("parallel",)),
        name="in_proj",
    )(x, w, cos, sin)


A_WINDOW_TILES = (A_LEFT_CHUNKS * CHUNK) // Q_BLOCK + 1
A_BLOCKS_PER_STEP = 4


def _band_bias(rel_bias):
    width = A_WINDOW_TILES * Q_BLOCK
    span = Q_BLOCK + width - 1
    d = np.arange(span) - (width - 1) + A_LEFT_CHUNKS * CHUNK
    gen = rel_bias[:, np.clip(d, -REL_CLIP, REL_CLIP) + REL_CLIP].astype(jnp.float32)
    sheared = jnp.tile(gen, (1, Q_BLOCK + 1))[:, :Q_BLOCK * (span + 1)].reshape(-1, Q_BLOCK, span + 1)
    bias = sheared[:, :, :width][:, :, ::-1]
    r = np.arange(Q_BLOCK)[:, None]
    e = np.arange(width)[None, :]
    back = r // CHUNK + A_LEFT_CHUNKS - e // CHUNK
    in_band = (back >= 0) & (back <= A_LEFT_CHUNKS)
    bias = jnp.where(jnp.asarray(in_band)[None], bias * LOG2_E, NEG_INF)
    return bias.reshape(A_HEADS // 2, 2, Q_BLOCK, width).transpose(0, 3, 1, 2).reshape(A_HEADS // 2, width, 2 * Q_BLOCK)


def _band_kernel(q_ref, k_ref, vt_ref, bias_ref, o_ref):
    even_head = lax.broadcasted_iota(jnp.int32, (Q_BLOCK, LANES), 1) < HEAD_DIM
    problems = []
    for sub in range(A_BLOCKS_PER_STEP):
        i = pl.program_id(1) * A_BLOCKS_PER_STEP + sub
        rows = slice(sub * Q_BLOCK, (sub + 1) * Q_BLOCK)
        tiles = []
        for tt in range(A_WINDOW_TILES):
            t = i - (A_WINDOW_TILES - 1) + tt
            tiles.append((t, jnp.maximum(t, 0)))
        for pair in range(A_HEADS // 2):
            problems.append((rows, slice(pair * LANES, (pair + 1) * LANES), pair, tiles))
    scores = []
    for rows, cols, pair, tiles in problems:
        qs = jnp.concatenate(_split_heads(q_ref[rows, cols], even_head), axis=0)
        scores.append([lax.dot_general(k_ref[pl.ds(pl.multiple_of(tc * Q_BLOCK, Q_BLOCK), Q_BLOCK), cols],
                                       qs, _NT, preferred_element_type=jnp.float32)
                       for t, tc in tiles])
    probs = []
    for (rows, cols, pair, tiles), s_raw in zip(problems, scores):
        s_tiles = [jnp.where(t >= 0, s + bias_ref[pair, tt * Q_BLOCK:(tt + 1) * Q_BLOCK, :], NEG_INF)
                   for tt, ((t, tc), s) in enumerate(zip(tiles, s_raw))]
        m = jnp.max(functools.reduce(jnp.maximum, s_tiles), axis=0, keepdims=True)
        p_tiles = [jnp.exp2(s - m) for s in s_tiles]
        l = jnp.sum(functools.reduce(jnp.add, p_tiles), axis=0, keepdims=True)
        probs.append(([p.astype(_MXU_DTYPE) for p in p_tiles], l))
    for (rows, cols, pair, tiles), (p_tiles, l) in zip(problems, probs):
        acc = jnp.zeros((LANES, 2 * Q_BLOCK), jnp.float32)
        for (t, tc), p in zip(tiles, p_tiles):
            acc = acc + jnp.dot(vt_ref[tc, cols, :], p, preferred_element_type=jnp.float32)
        acc = acc * (1.0 / l)
        out_t = jnp.concatenate([acc[:HEAD_DIM, :Q_BLOCK], acc[HEAD_DIM:, Q_BLOCK:]], axis=0)
        o_ref[rows, cols] = out_t.T.astype(o_ref.dtype)


def _band_attention(a_qk, avt, bias, bsz, seq):
    m = a_qk.shape[0]
    step_rows = A_BLOCKS_PER_STEP * Q_BLOCK
    nq = seq // step_rows
    return pl.pallas_call(
        _band_kernel,
        grid=(bsz, nq),
        in_specs=[
            pl.BlockSpec((step_rows, A_WIDTH), lambda b, i: (b * nq + i, 0)),
            pl.BlockSpec((seq, A_WIDTH), lambda b, i: (b, 1)),
            pl.BlockSpec((None, seq // Q_BLOCK, A_WIDTH, Q_BLOCK), lambda b, i: (b, 0, 0, 0)),
            _resident(bias.shape),
        ],
        out_specs=pl.BlockSpec((step_rows, A_WIDTH), lambda b, i: (b * nq + i, 0)),
        out_shape=jax.ShapeDtypeStruct((m, A_WIDTH), _MXU_DTYPE),
        compiler_params=_params(("parallel", "arbitrary")),
        name="band_attention",
    )(a_qk, a_qk, avt, bias)


def _diff_kernel(lam_ref, g_ref, q_ref, k_ref, vt_ref, o_ref, qs_ref, acc_ref, s_ref, p_ref,
                 *, lam_init, key_tile):
    i = pl.program_id(1)
    tk = key_tile
    bq = B_Q_BLOCK
    nq2 = 2 * bq
    lv = lam_ref[...]
    lam = (jnp.exp(jnp.sum(lv[0:1] * lv[1:2], axis=1, keepdims=True))
           - jnp.exp(jnp.sum(lv[2:3] * lv[3:4], axis=1, keepdims=True)) + lam_init)
    first_map = lax.broadcasted_iota(jnp.int32, (bq, LANES), 1) < HEAD_DIM
    for h in range(B_HEADS):
        qs_ref[h] = jnp.concatenate(_split_heads(q_ref[:, h * LANES:(h + 1) * LANES], first_map), axis=0)
    acc_ref[...] = jnp.zeros(acc_ref.shape, jnp.float32)
    col = lax.broadcasted_iota(jnp.int32, (1, nq2), 1)
    limit = i * bq + ((col & (bq - 1)) // CHUNK + 1) * CHUNK
    n_tiles = ((i + 1) * bq + tk - 1) // tk

    def fold(x, op):
        return op(x.reshape(tk // FOLD_ROWS, FOLD_ROWS, nq2), axis=0)

    def score_dots(t):
        k0 = pl.multiple_of(t * tk, tk)
        for h in range(B_HEADS):
            s_ref[h] = lax.dot_general(k_ref[pl.ds(k0, tk), h * LANES:(h + 1) * LANES], qs_ref[h], _NT,
                                       preferred_element_type=jnp.float32)

    def softmax_update(t, stats, masked):
        new_stats, weights = [], []
        for h in range(B_HEADS):
            s = s_ref[h]
            if masked:
                s = jnp.where(t * tk + lax.broadcasted_iota(jnp.int32, s.shape, 0) < limit, s, NEG_INF)
            m_old, l_old = stats[h]
            m_new = jnp.maximum(m_old, jnp.max(fold(s, jnp.max), axis=0, keepdims=True))
            alpha = jnp.exp2(m_old - m_new)
            p = jnp.exp2(s - m_new)
            new_stats.append((m_new, alpha * l_old + jnp.sum(fold(p, jnp.sum), axis=0, keepdims=True)))
            p_ref[h] = p.astype(_MXU_DTYPE)
            weights.append(alpha)
        return tuple(new_stats), weights

    def value_dots(t, weights):
        for h in range(B_HEADS):
            acc_ref[h] = weights[h] * acc_ref[h] + jnp.dot(vt_ref[t, h * B_V_DIM:(h + 1) * B_V_DIM, :], p_ref[h],
                                                      preferred_element_type=jnp.float32)

    def step(t, stats):
        stats, weights = softmax_update(t, stats, masked=False)
        score_dots(t + 1)
        value_dots(t, weights)
        return stats

    stats = tuple((jnp.full((1, nq2), NEG_INF, jnp.float32), jnp.zeros((1, nq2), jnp.float32))
                  for _ in range(B_HEADS))
    score_dots(0)
    stats = lax.fori_loop(0, n_tiles - 1, step, stats)
    stats, weights = softmax_update(n_tiles - 1, stats, masked=True)
    value_dots(n_tiles - 1, weights)

    for h in range(B_HEADS):
        acc = acc_ref[h] * (1.0 / stats[h][1])
        o = (acc[:, :bq] - acc[:, bq:] * lam).T
        ms = jnp.mean(o * o, axis=1, keepdims=True)
        o = o * lax.rsqrt(ms + NORM_EPS) * g_ref[...] * (1.0 - lam_init)
        o_ref[:, h * B_V_DIM:(h + 1) * B_V_DIM] = o.astype(o_ref.dtype)


def _diff_attention(bq, bk, bvt, lam_vecs, gain, lam_init, bsz, seq):
    m = bq.shape[0]
    nq = seq // B_Q_BLOCK
    n_kt, key_tile = bvt.shape[1], bvt.shape[3]
    return pl.pallas_call(
        functools.partial(_diff_kernel, lam_init=lam_init, key_tile=key_tile),
        grid=(bsz, nq),
        in_specs=[
            _resident(lam_vecs.shape),
            _resident(gain.shape),
            pl.BlockSpec((B_Q_BLOCK, B_HEADS * LANES), lambda b, i: (b * nq + i, 0)),
            pl.BlockSpec((seq, B_HEADS * LANES), lambda b, i: (b, 0)),
            pl.BlockSpec((None, n_kt, B_WIDTH, key_tile), lambda b, i: (b, 0, 0, 0)),
        ],
        out_specs=pl.BlockSpec((B_Q_BLOCK, B_WIDTH), lambda b, i: (b * nq + i, 0)),
        out_shape=jax.ShapeDtypeStruct((m, B_WIDTH), _MXU_DTYPE),
        scratch_shapes=[
            pltpu.VMEM((B_HEADS, 2 * B_Q_BLOCK, LANES), _MXU_DTYPE),
            pltpu.VMEM((B_HEADS, B_V_DIM, 2 * B_Q_BLOCK), jnp.float32),
            pltpu.VMEM((B_HEADS, key_tile, 2 * B_Q_BLOCK), jnp.float32),
            pltpu.VMEM((B_HEADS, key_tile, 2 * B_Q_BLOCK), _MXU_DTYPE),
        ],
        compiler_params=_params(("parallel", "arbitrary")),
        name="diff_attention",
    )(lam_vecs, gain, bq, bk, bvt)


C_BISECT_STEPS = 14


def _dsa_kernel(ik_ref, iq_ref, iw_ref, ck_ref, cvt_ref, cq_ref, o_ref,
                sc_ref, iqz_ref, qz_ref, acc_ref, s_ref, lg_ref, *, topk, key_tile, sel_tile, seq):
    i = pl.program_id(1)
    tk = key_tile
    cq = C_Q_BLOCK
    n_keys = (i + 1) * cq
    n_tiles = (n_keys + tk - 1) // tk
    n_sel = (n_keys + sel_tile - 1) // sel_tile
    lane_q = lax.broadcasted_iota(jnp.int32, (1, cq), 1)
    limit = i * cq + (lane_q // CHUNK + 1) * CHUNK
    searched = limit > topk
    even_head = lax.broadcasted_iota(jnp.int32, (cq, LANES), 1) < HEAD_DIM
    kk = float(topk)

    def fold(x, op=jnp.sum):
        return op(x.reshape(x.shape[0] // FOLD_ROWS, FOLD_ROWS, cq), axis=0)

    def key_pos(k0, rows):
        return k0 + lax.broadcasted_iota(jnp.int32, (rows, cq), 0)

    for j in range(IDX_HEADS // 2):
        iqz_ref[2 * j], iqz_ref[2 * j + 1] = _split_heads(iq_ref[:, j * LANES:(j + 1) * LANES], even_head)
    for j in range(C_HEADS // 2):
        qz_ref[2 * j], qz_ref[2 * j + 1] = _split_heads(cq_ref[:, j * LANES:(j + 1) * LANES], even_head)
    w = iw_ref[...]

    def index_dots(t, slot):
        ikt = ik_ref[pl.ds(pl.multiple_of(t * tk, tk), tk), :]
        for h in range(IDX_HEADS):
            lg_ref[slot, h] = lax.dot_general(ikt, iqz_ref[h], _NT, preferred_element_type=jnp.float32)

    def index_combine(t, slot, c):
        k0 = pl.multiple_of(t * tk, tk)
        acc = jnp.zeros((tk, cq), jnp.float32)
        for h in range(IDX_HEADS):
            acc = acc + jnp.maximum(lg_ref[slot, h], 0.0) * w[h:h + 1, :]
        adm = key_pos(k0, tk) < limit
        sc_ref[pl.ds(k0, tk), :] = jnp.where(adm, acc, NEG_INF)
        return (jnp.minimum(c[0], fold(jnp.where(adm, acc, jnp.inf), jnp.min)),
                jnp.maximum(c[1], fold(jnp.where(adm, acc, NEG_INF), jnp.max)))

    lo8, hi8 = _two_slot_pipeline(n_tiles, index_dots, index_combine,
                                  (jnp.full((FOLD_ROWS, cq), jnp.inf, jnp.float32),
                                   jnp.full((FOLD_ROWS, cq), NEG_INF, jnp.float32)))

    @pl.when(n_tiles * tk < n_sel * sel_tile)
    def _():
        k0 = pl.multiple_of(n_tiles * tk, tk)
        sc_ref[pl.ds(k0, tk), :] = jnp.full((tk, cq), NEG_INF, jnp.float32)

    def over_scores(fn, init):
        def body(t, acc):
            k0 = pl.multiple_of(t * sel_tile, sel_tile)
            return fn(sc_ref[pl.ds(k0, sel_tile), :], k0, acc)
        return lax.fori_loop(0, n_sel, body, init)

    def count(pred):
        def body(x, k0, acc):
            hit = pred(x, k0)
            for g in range(sel_tile // FOLD_ROWS):
                acc = jnp.where(hit[g * FOLD_ROWS:(g + 1) * FOLD_ROWS], acc + 1.0, acc)
            return acc
        part = over_scores(body, jnp.zeros((FOLD_ROWS, cq), jnp.float32))
        return jnp.sum(part, axis=0, keepdims=True)

    def count_ge(thr):
        return count(lambda x, k0: x >= thr)

    def max_where(pred):
        part = over_scores(lambda x, k0, acc: jnp.maximum(acc, fold(jnp.where(pred(x), x, NEG_INF), jnp.max)),
                           jnp.full((FOLD_ROWS, cq), NEG_INF, jnp.float32))
        return jnp.max(part, axis=0, keepdims=True)

    def write_mask(keep_fn):
        def body(t, carry):
            k0 = pl.multiple_of(t * tk, tk)
            x = sc_ref[pl.ds(k0, tk), :]
            kpos = key_pos(k0, tk)
            keep = jnp.logical_and(keep_fn(x, kpos), kpos < limit)
            sc_ref[pl.ds(k0, tk), :] = jnp.where(keep, 0.0, NEG_INF)
            return carry
        lax.fori_loop(0, n_tiles, body, 0)

    @pl.when(n_keys <= topk)
    def _():
        write_mask(lambda x, kpos: kpos >= 0)

    @pl.when(n_keys > topk)
    def _():
        lo = jnp.min(lo8, axis=0, keepdims=True)
        hi = jnp.max(hi8, axis=0, keepdims=True)

        def bisect(_, c):
            lo, hi = c
            mid = lo + (hi - lo) * 0.5
            ok = count_ge(mid) >= kk
            return jnp.where(ok, mid, lo), jnp.where(ok, hi, mid)

        lo, hi = lax.fori_loop(0, C_BISECT_STEPS, bisect, (lo, hi))

        def settled(cnt):
            return jnp.logical_or(cnt >= kk, jnp.logical_not(searched))

        thr = max_where(lambda x: x <= hi)
        cnt = count_ge(thr)

        def unsettled(c):
            return jnp.max(jnp.where(settled(c[1]), 0, 1)) > 0

        def walk(c):
            thr, cnt = c
            nxt = max_where(lambda x: x < thr)
            ncnt = count_ge(nxt)
            stay = settled(cnt)
            return jnp.where(stay, thr, nxt), jnp.where(stay, cnt, ncnt)

        thr, cnt = lax.while_loop(unsettled, walk, (thr, cnt))
        thr = jnp.where(searched, thr, NEG_INF)
        has_ties = jnp.max(jnp.where(jnp.logical_and(searched, cnt > kk), 1, 0)) > 0

        @pl.when(jnp.logical_not(has_ties))
        def _():
            write_mask(lambda x, kpos: x >= thr)

        @pl.when(has_ties)
        def _():
            need = kk - count(lambda x, k0: x > thr)
            cut = jnp.zeros((1, cq), jnp.int32)
            bit = seq
            while bit >= 1:
                cand = cut + bit
                before = count(lambda x, k0: jnp.logical_and(x == thr, key_pos(k0, sel_tile) < cand))
                cut = jnp.where(before <= need, cand, cut)
                bit //= 2
            cut = jnp.where(searched, cut, 2 * seq)
            write_mask(lambda x, kpos: jnp.logical_or(x > thr, jnp.logical_and(x == thr, kpos < cut)))

    acc_ref[...] = jnp.zeros(acc_ref.shape, jnp.float32)

    def score_dots(t, slot):
        k0 = pl.multiple_of(t * tk, tk)
        for h in range(C_HEADS):
            s_ref[slot, h] = lax.dot_general(ck_ref[pl.ds(k0, tk), (h // 2) * LANES:(h // 2 + 1) * LANES],
                                             qz_ref[h], _NT, preferred_element_type=jnp.float32)

    def attend(t, slot, stats):
        mask = sc_ref[pl.ds(pl.multiple_of(t * tk, tk), tk), :]
        new_stats, weights = [], []
        for h in range(C_HEADS):
            m_old, l_old = stats[h]
            s = s_ref[slot, h] + mask
            m_new = jnp.maximum(m_old, jnp.max(fold(s, jnp.max), axis=0, keepdims=True))
            alpha = jnp.exp2(m_old - m_new)
            p = jnp.exp2(s - m_new)
            new_stats.append((m_new, alpha * l_old + jnp.sum(fold(p), axis=0, keepdims=True)))
            weights.append((alpha, p.astype(_MXU_DTYPE)))
        for h in range(C_HEADS):
            alpha, p = weights[h]
            acc_ref[h] = alpha * acc_ref[h] + jnp.dot(cvt_ref[t, h * HEAD_DIM:(h + 1) * HEAD_DIM, :], p,
                                                      preferred_element_type=jnp.float32)
        return tuple(new_stats)

    stats = tuple((jnp.full((1, cq), NEG_INF, jnp.float32), jnp.zeros((1, cq), jnp.float32))
                  for _ in range(C_HEADS))
    stats = _two_slot_pipeline(n_tiles, score_dots, attend, stats)
    out = jnp.concatenate([acc_ref[h] * (1.0 / stats[h][1]) for h in range(C_HEADS)], axis=0)
    o_ref[...] = out.T.astype(o_ref.dtype)


def _dsa_attention(ik2, iq, iwt, ck, cvt, cq, bsz, seq, key_tile):
    m = iq.shape[0]
    cqb = C_Q_BLOCK
    nq = seq // cqb
    topk = min(C_TOPK_MAX, seq // 4)
    sel_tile = min(seq, 512)
    assert key_tile == cqb and seq % sel_tile == 0 and sel_tile % key_tile == 0
    return pl.pallas_call(
        functools.partial(_dsa_kernel, topk=topk, key_tile=key_tile, sel_tile=sel_tile, seq=seq),
        grid=(bsz, nq),
        in_specs=[
            pl.BlockSpec((seq, LANES), lambda b, i: (b, 0)),
            pl.BlockSpec((cqb, IDX_HEADS * IDX_DIM), lambda b, i: (b * nq + i, 0)),
            pl.BlockSpec((None, IDX_HEADS, cqb), lambda b, i: (b, 0, i)),
            pl.BlockSpec((seq, C_WIDTH), lambda b, i: (b, 0)),
            pl.BlockSpec((None, seq // key_tile, C_WIDTH, key_tile), lambda b, i: (b, 0, 0, 0)),
            pl.BlockSpec((cqb, C_WIDTH), lambda b, i: (b * nq + i, 0)),
        ],
        out_specs=pl.BlockSpec((cqb, C_WIDTH), lambda b, i: (b * nq + i, 0)),
        out_shape=jax.ShapeDtypeStruct((m, C_WIDTH), _MXU_DTYPE),
        scratch_shapes=[
            pltpu.VMEM((seq, cqb), jnp.float32),
            pltpu.VMEM((IDX_HEADS, cqb, LANES), _MXU_DTYPE),
            pltpu.VMEM((C_HEADS, cqb, LANES), _MXU_DTYPE),
            pltpu.VMEM((C_HEADS, HEAD_DIM, cqb), jnp.float32),
            pltpu.VMEM((2, C_HEADS, key_tile, cqb), jnp.float32),
            pltpu.VMEM((2, IDX_HEADS, key_tile, cqb), jnp.float32),
        ],
        compiler_params=_params(("parallel", "arbitrary")),
        name="dsa_attention",
    )(ik2, iq, iwt, ck, cvt, cq)


def _layer_norm(z, g, b):
    mu = jnp.mean(z, axis=-1, keepdims=True)
    zc = z - mu
    var = jnp.mean(zc * zc, axis=-1, keepdims=True)
    return zc * lax.rsqrt(var + NORM_EPS) * g + b


def _post_kernel(x_ref, oa_ref, ob_ref, oc_ref, p_ref, wo_ref, wup_ref, wdown_ref, wgate_ref, wple_ref,
                 g1_ref, b1_ref, g2_ref, b2_ref, y_ref, *, alpha, ff_tile):
    y = jnp.dot(oa_ref[...], wo_ref[0:A_WIDTH, :], preferred_element_type=jnp.float32)
    y = y + jnp.dot(ob_ref[...], wo_ref[A_WIDTH:A_WIDTH + B_WIDTH, :], preferred_element_type=jnp.float32)
    y = y + jnp.dot(oc_ref[...], wo_ref[A_WIDTH + B_WIDTH:, :], preferred_element_type=jnp.float32)
    emb = jnp.dot(p_ref[...].astype(_MXU_DTYPE), wple_ref[...], preferred_element_type=jnp.float32)
    x1 = _layer_norm(alpha * x_ref[...] + y, g1_ref[...], b1_ref[...])
    x1b = x1.astype(_MXU_DTYPE)
    ff = jnp.zeros(x1.shape, jnp.float32)
    for f0 in range(0, wup_ref.shape[1], ff_tile):
        u = jnp.dot(x1b, wup_ref[:, f0:f0 + ff_tile], preferred_element_type=jnp.float32)
        u = jnp.square(jnp.maximum(u, 0.0)).astype(_MXU_DTYPE)
        ff = ff + jnp.dot(u, wdown_ref[f0:f0 + ff_tile, :], preferred_element_type=jnp.float32)
    gate = jax.nn.sigmoid(jnp.dot(x1b, wgate_ref[...], preferred_element_type=jnp.float32))
    y_ref[...] = _layer_norm(alpha * x1 + ff + gate * emb, g2_ref[...], b2_ref[...])


def _post(x, oa, ob, oc, p_all, layer, wo, wup, wdown, wgate, wple, g1, b1, g2, b2, alpha, seq):
    m, d = x.shape
    tm = min(seq, 512)
    steps = m // tm

    def rows(width):
        return pl.BlockSpec((tm, width), lambda i: (i, 0))

    weights = [wo, wup, wdown, wgate, wple, g1, b1, g2, b2]
    return pl.pallas_call(
        functools.partial(_post_kernel, alpha=alpha, ff_tile=512),
        grid=(steps,),
        in_specs=[rows(d), rows(A_WIDTH), rows(B_WIDTH), rows(C_WIDTH),
                  pl.BlockSpec((tm, p_all.shape[1]), lambda i: (layer * steps + i, 0))]
                 + [_resident(t.shape) for t in weights],
        out_specs=rows(d),
        out_shape=jax.ShapeDtypeStruct((m, d), jnp.float32),
        compiler_params=_params(("parallel",)),
        name="post_mlp",
    )(x, oa, ob, oc, p_all, *weights)


def kernel(x, p, positions, w_in, rel_bias, lam_q1, lam_k1, lam_q2, lam_k2, diff_norm_g, w_o,
           ln1_g, ln1_b, w_up, w_down, w_ple_gate, w_ple, ln2_g, ln2_b):
    bsz, seq, d_model = x.shape
    depth = w_in.shape[0]
    m = bsz * seq
    assert seq % C_Q_BLOCK == 0 and w_in.shape[2] == IN_TOTAL
    key_tile = C_Q_BLOCK
    alpha = (2 * depth) ** 0.25
    cast = lambda t: t.astype(_MXU_DTYPE)

    cos, sin = _rope_tables(positions)
    xf = x.reshape(m, d_model)
    for i in range(depth):
        w = jnp.pad(cast(w_in[i]), ((0, 0), (0, IN_PADDED - IN_TOTAL)))
        a_qk, avt, bq, bk, bvt, cq, ck, cvt, iq, ik2, iwt = _in_proj(xf, w, cos, sin, bsz, seq, key_tile)
        lam_init = 0.8 - 0.6 * math.exp(-0.3 * i)
        lam_vecs = jnp.stack([lam_q1[i], lam_k1[i], lam_q2[i], lam_k2[i]]).astype(jnp.float32)
        o_a = _band_attention(a_qk, avt, _band_bias(rel_bias[i]), bsz, seq)
        o_b = _diff_attention(bq, bk, bvt, lam_vecs, diff_norm_g[i][None, :].astype(jnp.float32),
                              lam_init, bsz, seq)
        o_c = _dsa_attention(ik2, iq, iwt, ck, cvt, cq, bsz, seq, key_tile)
        row = lambda t: t[i][None, :].astype(jnp.float32)
        xf = _post(xf, o_a, o_b, o_c, p.reshape(depth * m, p.shape[-1]), i,
                   cast(w_o[i]), cast(w_up[i]), cast(w_down[i]), cast(w_ple_gate[i]), cast(w_ple[i]),
                   row(ln1_g), row(ln1_b), row(ln2_g), row(ln2_b), alpha, seq)
    return xf.reshape(bsz, seq, d_model)
```

```python
import functools
import math

import jax
import jax.numpy as jnp
import numpy as np
from jax import lax
from jax.experimental import pallas as pl
from jax.experimental.pallas import tpu as pltpu

CHUNK = 64
HEAD_DIM = 64
A_HEADS = 4
A_LEFT_CHUNKS = 8
REL_CLIP = 128
B_HEADS = 4
B_V_DIM = 2 * HEAD_DIM
C_HEADS = 4
C_TOPK_MAX = 256
IDX_HEADS = 8
IDX_DIM = 64
A_WIDTH = A_HEADS * HEAD_DIM
B_WIDTH = B_HEADS * B_V_DIM
C_WIDTH = C_HEADS * HEAD_DIM
ROPE_THETA = 10000.0
NORM_EPS = 1e-5
NEG_INF = -1e30
LOG2_E = math.log2(math.e)

LANES = 128
FOLD_ROWS = 64
Q_BLOCK = 128
B_Q_BLOCK = 512
B_KEY_TILE = 512
C_Q_BLOCK = 256
VMEM_LIMIT = 56 * 1024 * 1024

OFF_A = 0
OFF_BQ = 3 * A_WIDTH
OFF_BK = OFF_BQ + 2 * B_HEADS * HEAD_DIM
OFF_BV = OFF_BK + 2 * B_HEADS * HEAD_DIM
OFF_CQ = OFF_BV + B_WIDTH
OFF_CK = OFF_CQ + C_WIDTH
OFF_CV = OFF_CK + C_WIDTH
OFF_IQ = OFF_CV + C_WIDTH
OFF_IK = OFF_IQ + IDX_HEADS * IDX_DIM
IN_TOTAL = OFF_IK + IDX_DIM + IDX_HEADS
IN_PADDED = OFF_IK + LANES

_MXU_DTYPE = jnp.bfloat16
_NT = (((1,), (1,)), ((), ()))


def _params(sem, **flags):
    return pltpu.CompilerParams(dimension_semantics=sem, vmem_limit_bytes=VMEM_LIMIT, flags=flags or None)


def _resident(shape):
    return pl.BlockSpec(shape, lambda *_: (0,) * len(shape), pipeline_mode=pl.Buffered(1))


def _two_slot_pipeline(n_tiles, produce, consume, carry, consume_last=None):
    consume_last = consume_last or consume
    n_pairs = (n_tiles - 1) // 2
    produce(0, 0)

    def pair(j, c):
        t0 = 2 * j
        produce(t0 + 1, 1)
        c = consume(t0, 0, c)
        produce(t0 + 2, 0)
        return consume(t0 + 1, 1, c)

    carry = lax.fori_loop(0, n_pairs, pair, carry)
    t0 = 2 * n_pairs

    def one(c):
        return consume_last(t0, 0, c)

    def two(c):
        produce(t0 + 1, 1)
        return consume_last(t0 + 1, 1, consume(t0, 0, c))

    return lax.cond(n_tiles - t0 == 1, one, two, carry)


def _split_heads(t, even):
    zero = jnp.zeros((), t.dtype)
    return jnp.where(even, t, zero), jnp.where(even, zero, t)


def _rope_table_kernel(pos_ref, inv_ref, sgn_ref, cos_ref, sin_ref):
    ang = pos_ref[...].astype(jnp.float32) * inv_ref[...]
    cos_ref[...] = jnp.cos(ang)
    sin_ref[...] = jnp.sin(ang) * sgn_ref[...]


def _rope_tables(positions):
    m = positions.size
    tm = min(m, 2048)
    inv = ROPE_THETA ** (-jnp.arange(0, HEAD_DIM, 2, dtype=jnp.float32) / HEAD_DIM)
    inv = jnp.tile(inv, LANES // (HEAD_DIM // 2))[None, :]
    sgn = jnp.tile(jnp.concatenate([-jnp.ones(HEAD_DIM // 2), jnp.ones(HEAD_DIM // 2)]),
                   LANES // HEAD_DIM).astype(jnp.float32)[None, :]
    row = pl.BlockSpec((tm, LANES), lambda i: (i, 0))
    const = pl.BlockSpec((1, LANES), lambda i: (0, 0))
    return pl.pallas_call(
        _rope_table_kernel,
        grid=(m // tm,),
        in_specs=[pl.BlockSpec((tm, 1), lambda i: (i, 0)), const, const],
        out_specs=[row, row],
        out_shape=[jax.ShapeDtypeStruct((m, LANES), jnp.float32)] * 2,
        compiler_params=_params(("parallel",)),
        name="rope_tables",
    )(positions.reshape(m, 1), inv, sgn)


def _in_proj_kernel(x_ref, w_ref, cos_ref, sin_ref,
                    a_ref, avt_ref, bq_ref, bk_ref, bvt_ref, cq_ref, ck_ref, cvt_ref,
                    iq_ref, ik2_ref, iwt_ref, *, key_tile):
    tm = x_ref.shape[0]
    xb = x_ref[...].astype(_MXU_DTYPE)
    cos = cos_ref[...]
    sin = sin_ref[...]
    lane = lax.broadcasted_iota(jnp.int32, (tm, LANES), 1)
    low_half = (lane & (HEAD_DIM - 1)) < HEAD_DIM // 2
    qk_scale = HEAD_DIM ** -0.5 * LOG2_E

    def proj(c0, n):
        return jnp.dot(xb, w_ref[:, c0:c0 + n], preferred_element_type=jnp.float32)

    def rope(t):
        rot = jnp.where(low_half, pltpu.roll(t, LANES - HEAD_DIM // 2, 1),
                        pltpu.roll(t, HEAD_DIM // 2, 1))
        return t * cos + rot * sin

    def store(ref, h, roped=False, scale=None, col0=0):
        for j in range(h.shape[1] // LANES):
            t = h[:, j * LANES:(j + 1) * LANES]
            if roped:
                t = rope(t)
            if scale is not None:
                t = t * scale
            ref[:, col0 + j * LANES:col0 + (j + 1) * LANES] = t.astype(ref.dtype)

    store(a_ref, proj(OFF_A, A_WIDTH), scale=qk_scale)
    store(a_ref, proj(OFF_A + A_WIDTH, A_WIDTH), col0=A_WIDTH)
    av = proj(OFF_A + 2 * A_WIDTH, A_WIDTH)
    for c in range(tm // Q_BLOCK):
        avt_ref[c] = av[c * Q_BLOCK:(c + 1) * Q_BLOCK, :].T.astype(avt_ref.dtype)
    store(bq_ref, proj(OFF_BQ, OFF_BK - OFF_BQ), roped=True, scale=qk_scale)
    store(bk_ref, proj(OFF_BK, OFF_BV - OFF_BK), roped=True)
    bv = proj(OFF_BV, B_WIDTH)
    b_tile = bvt_ref.shape[-1]
    for c in range(tm // b_tile):
        bvt_ref[c] = bv[c * b_tile:(c + 1) * b_tile, :].T.astype(bvt_ref.dtype)
    store(cq_ref, proj(OFF_CQ, C_WIDTH), roped=True, scale=qk_scale)
    store(ck_ref, proj(OFF_CK, C_WIDTH), roped=True)
    cv = proj(OFF_CV, C_WIDTH)
    for c in range(tm // key_tile):
        cvt_ref[c] = cv[c * key_tile:(c + 1) * key_tile, :].T.astype(cvt_ref.dtype)
    store(iq_ref, proj(OFF_IQ, IDX_HEADS * IDX_DIM), roped=True)
    last = proj(OFF_IK, LANES)
    ikr = rope(last)
    ik2 = jnp.where(lane < IDX_DIM, ikr, pltpu.roll(ikr, IDX_DIM, 1))
    ik2_ref[...] = ik2.astype(ik2_ref.dtype)
    iwt_ref[...] = last.T[IDX_DIM:IDX_DIM + IDX_HEADS, :] * ((IDX_HEADS * IDX_DIM) ** -0.5)


def _in_proj(x, w, cos, sin, bsz, seq, key_tile):
    m, d = x.shape
    tm = min(seq, 1024)
    b_tile = min(seq, B_KEY_TILE)
    n_s = seq // tm
    f = _MXU_DTYPE

    def rows(width):
        return pl.BlockSpec((tm, width), lambda i: (i, 0))

    out_shape = [
        jax.ShapeDtypeStruct((m, 2 * A_WIDTH), f),
        jax.ShapeDtypeStruct((bsz, seq // Q_BLOCK, A_WIDTH, Q_BLOCK), f),
        jax.ShapeDtypeStruct((m, OFF_BK - OFF_BQ), f),
        jax.ShapeDtypeStruct((m, OFF_BV - OFF_BK), f),
        jax.ShapeDtypeStruct((bsz, seq // b_tile, B_WIDTH, b_tile), f),
        jax.ShapeDtypeStruct((m, C_WIDTH), f),
        jax.ShapeDtypeStruct((m, C_WIDTH), f),
        jax.ShapeDtypeStruct((bsz, seq // key_tile, C_WIDTH, key_tile), f),
        jax.ShapeDtypeStruct((m, IDX_HEADS * IDX_DIM), f),
        jax.ShapeDtypeStruct((m, LANES), f),
        jax.ShapeDtypeStruct((bsz, IDX_HEADS, seq), jnp.float32),
    ]
    out_specs = [
        rows(2 * A_WIDTH),
        pl.BlockSpec((None, tm // Q_BLOCK, A_WIDTH, Q_BLOCK), lambda i: (i // n_s, i % n_s, 0, 0)),
        rows(OFF_BK - OFF_BQ), rows(OFF_BV - OFF_BK),
        pl.BlockSpec((None, tm // b_tile, B_WIDTH, b_tile), lambda i: (i // n_s, i % n_s, 0, 0)),
        rows(C_WIDTH), rows(C_WIDTH),
        pl.BlockSpec((None, tm // key_tile, C_WIDTH, key_tile), lambda i: (i // n_s, i % n_s, 0, 0)),
        rows(IDX_HEADS * IDX_DIM), rows(LANES),
        pl.BlockSpec((None, IDX_HEADS, tm), lambda i: (i // n_s, 0, i % n_s)),
    ]
    return pl.pallas_call(
        functools.partial(_in_proj_kernel, key_tile=key_tile),
        grid=(m // tm,),
        in_specs=[rows(d), _resident(w.shape), rows(LANES), rows(LANES)],
        out_specs=out_specs,
        out_shape=out_shape,
        compiler_params=_params(("parallel",)),
        name="in_proj",
    )(x, w, cos, sin)


A_WINDOW_TILES = (A_LEFT_CHUNKS * CHUNK) // Q_BLOCK + 1
A_BLOCKS_PER_STEP = 4


def _band_bias(rel_bias):
    width = A_WINDOW_TILES * Q_BLOCK
    span = Q_BLOCK + width - 1
    d = np.arange(span) - (width - 1) + A_LEFT_CHUNKS * CHUNK
    gen = rel_bias[:, np.clip(d, -REL_CLIP, REL_CLIP) + REL_CLIP].astype(jnp.float32)
    sheared = jnp.tile(gen, (1, Q_BLOCK + 1))[:, :Q_BLOCK * (span + 1)].reshape(-1, Q_BLOCK, span + 1)
    bias = sheared[:, :, :width][:, :, ::-1]
    r = np.arange(Q_BLOCK)[:, None]
    e = np.arange(width)[None, :]
    back = r // CHUNK + A_LEFT_CHUNKS - e // CHUNK
    in_band = (back >= 0) & (back <= A_LEFT_CHUNKS)
    bias = jnp.where(jnp.asarray(in_band)[None], bias * LOG2_E, NEG_INF)
    return bias.reshape(A_HEADS // 2, 2, Q_BLOCK, width).transpose(0, 3, 1, 2).reshape(A_HEADS // 2, width, 2 * Q_BLOCK)


def _band_kernel(q_ref, k_ref, vt_ref, bias_ref, o_ref):
    even_head = lax.broadcasted_iota(jnp.int32, (Q_BLOCK, LANES), 1) < HEAD_DIM
    problems = []
    for sub in range(A_BLOCKS_PER_STEP):
        i = pl.program_id(1) * A_BLOCKS_PER_STEP + sub
        rows = slice(sub * Q_BLOCK, (sub + 1) * Q_BLOCK)
        tiles = []
        for tt in range(A_WINDOW_TILES):
            t = i - (A_WINDOW_TILES - 1) + tt
            tiles.append((t, jnp.maximum(t, 0)))
        for pair in range(A_HEADS // 2):
            problems.append((rows, slice(pair * LANES, (pair + 1) * LANES), pair, tiles))
    scores = []
    for rows, cols, pair, tiles in problems:
        qs = jnp.concatenate(_split_heads(q_ref[rows, cols], even_head), axis=0)
        scores.append([lax.dot_general(k_ref[pl.ds(pl.multiple_of(tc * Q_BLOCK, Q_BLOCK), Q_BLOCK), cols],
                                       qs, _NT, preferred_element_type=jnp.float32)
                       for t, tc in tiles])
    probs = []
    for (rows, cols, pair, tiles), s_raw in zip(problems, scores):
        s_tiles = [jnp.where(t >= 0, s + bias_ref[pair, tt * Q_BLOCK:(tt + 1) * Q_BLOCK, :], NEG_INF)
                   for tt, ((t, tc), s) in enumerate(zip(tiles, s_raw))]
        m = jnp.max(functools.reduce(jnp.maximum, s_tiles), axis=0, keepdims=True)
        p_tiles = [jnp.exp2(s - m) for s in s_tiles]
        l = jnp.sum(functools.reduce(jnp.add, p_tiles), axis=0, keepdims=True)
        probs.append(([p.astype(_MXU_DTYPE) for p in p_tiles], l))
    for (rows, cols, pair, tiles), (p_tiles, l) in zip(problems, probs):
        acc = jnp.zeros((LANES, 2 * Q_BLOCK), jnp.float32)
        for (t, tc), p in zip(tiles, p_tiles):
            acc = acc + jnp.dot(vt_ref[tc, cols, :], p, preferred_element_type=jnp.float32)
        acc = acc * (1.0 / l)
        out_t = jnp.concatenate([acc[:HEAD_DIM, :Q_BLOCK], acc[HEAD_DIM:, Q_BLOCK:]], axis=0)
        o_ref[rows, cols] = out_t.T.astype(o_ref.dtype)


def _band_attention(a_qk, avt, bias, bsz, seq):
    m = a_qk.shape[0]
    step_rows = A_BLOCKS_PER_STEP * Q_BLOCK
    nq = seq // step_rows
    return pl.pallas_call(
        _band_kernel,
        grid=(bsz, nq),
        in_specs=[
            pl.BlockSpec((step_rows, A_WIDTH), lambda b, i: (b * nq + i, 0)),
            pl.BlockSpec((seq, A_WIDTH), lambda b, i: (b, 1)),
            pl.BlockSpec((None, seq // Q_BLOCK, A_WIDTH, Q_BLOCK), lambda b, i: (b, 0, 0, 0)),
            _resident(bias.shape),
        ],
        out_specs=pl.BlockSpec((step_rows, A_WIDTH), lambda b, i: (b * nq + i, 0)),
        out_shape=jax.ShapeDtypeStruct((m, A_WIDTH), _MXU_DTYPE),
        compiler_params=_params(("parallel", "arbitrary")),
        name="band_attention",
    )(a_qk, a_qk, avt, bias)


def _diff_kernel(lam_ref, g_ref, q_ref, k_ref, vt_ref, o_ref, qs_ref, acc_ref, s_ref, p_ref,
                 *, lam_init, key_tile):
    i = pl.program_id(1)
    tk = key_tile
    bq = B_Q_BLOCK
    nq2 = 2 * bq
    lv = lam_ref[...]
    lam = (jnp.exp(jnp.sum(lv[0:1] * lv[1:2], axis=1, keepdims=True))
           - jnp.exp(jnp.sum(lv[2:3] * lv[3:4], axis=1, keepdims=True)) + lam_init)
    first_map = lax.broadcasted_iota(jnp.int32, (bq, LANES), 1) < HEAD_DIM
    for h in range(B_HEADS):
        qs_ref[h] = jnp.concatenate(_split_heads(q_ref[:, h * LANES:(h + 1) * LANES], first_map), axis=0)
    acc_ref[...] = jnp.zeros(acc_ref.shape, jnp.float32)
    col = lax.broadcasted_iota(jnp.int32, (1, nq2), 1)
    limit = i * bq + ((col & (bq - 1)) // CHUNK + 1) * CHUNK
    n_tiles = ((i + 1) * bq + tk - 1) // tk

    def fold(x, op):
        return op(x.reshape(tk // FOLD_ROWS, FOLD_ROWS, nq2), axis=0)

    def score_dots(t):
        k0 = pl.multiple_of(t * tk, tk)
        for h in range(B_HEADS):
            s_ref[h] = lax.dot_general(k_ref[pl.ds(k0, tk), h * LANES:(h + 1) * LANES], qs_ref[h], _NT,
                                       preferred_element_type=jnp.float32)

    def softmax_update(t, stats, masked):
        new_stats, weights = [], []
        for h in range(B_HEADS):
            s = s_ref[h]
            if masked:
                s = jnp.where(t * tk + lax.broadcasted_iota(jnp.int32, s.shape, 0) < limit, s, NEG_INF)
            m_old, l_old = stats[h]
            m_new = jnp.maximum(m_old, jnp.max(fold(s, jnp.max), axis=0, keepdims=True))
            alpha = jnp.exp2(m_old - m_new)
            p = jnp.exp2(s - m_new)
            new_stats.append((m_new, alpha * l_old + jnp.sum(fold(p, jnp.sum), axis=0, keepdims=True)))
            p_ref[h] = p.astype(_MXU_DTYPE)
            weights.append(alpha)
        return tuple(new_stats), weights

    def value_dots(t, weights):
        for h in range(B_HEADS):
            acc_ref[h] = weights[h] * acc_ref[h] + jnp.dot(vt_ref[t, h * B_V_DIM:(h + 1) * B_V_DIM, :], p_ref[h],
                                                      preferred_element_type=jnp.float32)

    def step(t, stats):
        stats, weights = softmax_update(t, stats, masked=False)
        score_dots(t + 1)
        value_dots(t, weights)
        return stats

    stats = tuple((jnp.full((1, nq2), NEG_INF, jnp.float32), jnp.zeros((1, nq2), jnp.float32))
                  for _ in range(B_HEADS))
    score_dots(0)
    stats = lax.fori_loop(0, n_tiles - 1, step, stats)
    stats, weights = softmax_update(n_tiles - 1, stats, masked=True)
    value_dots(n_tiles - 1, weights)

    for h in range(B_HEADS):
        acc = acc_ref[h] * (1.0 / stats[h][1])
        o = (acc[:, :bq] - acc[:, bq:] * lam).T
        ms = jnp.mean(o * o, axis=1, keepdims=True)
        o = o * lax.rsqrt(ms + NORM_EPS) * g_ref[...] * (1.0 - lam_init)
        o_ref[:, h * B_V_DIM:(h + 1) * B_V_DIM] = o.astype(o_ref.dtype)


def _diff_attention(bq, bk, bvt, lam_vecs, gain, lam_init, bsz, seq):
    m = bq.shape[0]
    nq = seq // B_Q_BLOCK
    n_kt, key_tile = bvt.shape[1], bvt.shape[3]
    return pl.pallas_call(
        functools.partial(_diff_kernel, lam_init=lam_init, key_tile=key_tile),
        grid=(bsz, nq),
        in_specs=[
            _resident(lam_vecs.shape),
            _resident(gain.shape),
            pl.BlockSpec((B_Q_BLOCK, B_HEADS * LANES), lambda b, i: (b * nq + i, 0)),
            pl.BlockSpec((seq, B_HEADS * LANES), lambda b, i: (b, 0)),
            pl.BlockSpec((None, n_kt, B_WIDTH, key_tile), lambda b, i: (b, 0, 0, 0)),
        ],
        out_specs=pl.BlockSpec((B_Q_BLOCK, B_WIDTH), lambda b, i: (b * nq + i, 0)),
        out_shape=jax.ShapeDtypeStruct((m, B_WIDTH), _MXU_DTYPE),
        scratch_shapes=[
            pltpu.VMEM((B_HEADS, 2 * B_Q_BLOCK, LANES), _MXU_DTYPE),
            pltpu.VMEM((B_HEADS, B_V_DIM, 2 * B_Q_BLOCK), jnp.float32),
            pltpu.VMEM((B_HEADS, key_tile, 2 * B_Q_BLOCK), jnp.float32),
            pltpu.VMEM((B_HEADS, key_tile, 2 * B_Q_BLOCK), _MXU_DTYPE),
        ],
        compiler_params=_params(("parallel", "arbitrary")),
        name="diff_attention",
    )(lam_vecs, gain, bq, bk, bvt)


C_BISECT_STEPS = 14


def _dsa_kernel(ik_ref, iq_ref, iw_ref, ck_ref, cvt_ref, cq_ref, o_ref,
                sc_ref, iqz_ref, qz_ref, acc_ref, s_ref, lg_ref, kth_ref, *, topk, key_tile, sel_tile, seq):
    i = pl.program_id(1)
    tk = key_tile
    cq = C_Q_BLOCK
    n_keys = (i + 1) * cq
    n_tiles = (n_keys + tk - 1) // tk
    n_sel = (n_keys + sel_tile - 1) // sel_tile
    lane_q = lax.broadcasted_iota(jnp.int32, (1, cq), 1)
    limit = i * cq + (lane_q // CHUNK + 1) * CHUNK
    searched = limit > topk
    even_head = lax.broadcasted_iota(jnp.int32, (cq, LANES), 1) < HEAD_DIM
    kk = float(topk)

    def fold(x, op=jnp.sum):
        return op(x.reshape(x.shape[0] // FOLD_ROWS, FOLD_ROWS, cq), axis=0)

    def key_pos(k0, rows):
        return k0 + lax.broadcasted_iota(jnp.int32, (rows, cq), 0)

    for j in range(IDX_HEADS // 2):
        iqz_ref[2 * j], iqz_ref[2 * j + 1] = _split_heads(iq_ref[:, j * LANES:(j + 1) * LANES], even_head)
    for j in range(C_HEADS // 2):
        qz_ref[2 * j], qz_ref[2 * j + 1] = _split_heads(cq_ref[:, j * LANES:(j + 1) * LANES], even_head)
    w = iw_ref[...]

    def index_dots(t, slot):
        ikt = ik_ref[pl.ds(pl.multiple_of(t * tk, tk), tk), :]
        for h in range(IDX_HEADS):
            lg_ref[slot, h] = lax.dot_general(ikt, iqz_ref[h], _NT, preferred_element_type=jnp.float32)

    def index_combine(t, slot, c):
        k0 = pl.multiple_of(t * tk, tk)
        acc = jnp.zeros((tk, cq), jnp.float32)
        for h in range(IDX_HEADS):
            acc = acc + jnp.maximum(lg_ref[slot, h], 0.0) * w[h:h + 1, :]
        adm = key_pos(k0, tk) < limit
        sc_ref[pl.ds(k0, tk), :] = jnp.where(adm, acc, NEG_INF)
        return (jnp.minimum(c[0], fold(jnp.where(adm, acc, jnp.inf), jnp.min)),
                jnp.maximum(c[1], fold(jnp.where(adm, acc, NEG_INF), jnp.max)))

    lo8, hi8 = _two_slot_pipeline(n_tiles, index_dots, index_combine,
                                  (jnp.full((FOLD_ROWS, cq), jnp.inf, jnp.float32),
                                   jnp.full((FOLD_ROWS, cq), NEG_INF, jnp.float32)))

    @pl.when(n_tiles * tk < n_sel * sel_tile)
    def _():
        k0 = pl.multiple_of(n_tiles * tk, tk)
        sc_ref[pl.ds(k0, tk), :] = jnp.full((tk, cq), NEG_INF, jnp.float32)

    class _Passes:
        def __init__(self, n_chunks):
            self.n_chunks = n_chunks

        def over_scores(self, fn, init):
            if self.n_chunks is None:
                def body(t, acc):
                    k0 = pl.multiple_of(t * sel_tile, sel_tile)
                    return fn(sc_ref[pl.ds(k0, sel_tile), :], k0, acc)
                return lax.fori_loop(0, n_sel, body, init)
            acc = init
            for c in range(self.n_chunks):
                acc = fn(sc_ref[c * sel_tile:(c + 1) * sel_tile, :], c * sel_tile, acc)
            return acc

        def count(self, pred):
            def body(x, k0, acc):
                hit = pred(x, k0)
                for g in range(sel_tile // FOLD_ROWS):
                    acc = jnp.where(hit[g * FOLD_ROWS:(g + 1) * FOLD_ROWS], acc + 1.0, acc)
                return acc
            part = self.over_scores(body, jnp.zeros((FOLD_ROWS, cq), jnp.float32))
            return jnp.sum(part, axis=0, keepdims=True)

        def count_ge(self, thr):
            return self.count(lambda x, k0: x >= thr)

        def max_where(self, pred):
            part = self.over_scores(
                lambda x, k0, acc: jnp.maximum(acc, fold(jnp.where(pred(x), x, NEG_INF), jnp.max)),
                jnp.full((FOLD_ROWS, cq), NEG_INF, jnp.float32))
            return jnp.max(part, axis=0, keepdims=True)

    def write_mask(keep_fn):
        def body(t, carry):
            k0 = pl.multiple_of(t * tk, tk)
            x = sc_ref[pl.ds(k0, tk), :]
            kpos = key_pos(k0, tk)
            keep = jnp.logical_and(keep_fn(x, kpos), kpos < limit)
            sc_ref[pl.ds(k0, tk), :] = jnp.where(keep, 0.0, NEG_INF)
            return carry
        lax.fori_loop(0, n_tiles, body, 0)

    def settled(cnt):
        return jnp.logical_or(cnt >= kk, jnp.logical_not(searched))

    def kth_largest(passes):
        lo = jnp.min(lo8, axis=0, keepdims=True)
        hi = jnp.max(hi8, axis=0, keepdims=True)

        def bisect(_, c):
            lo, hi = c
            mid = lo + (hi - lo) * 0.5
            ok = passes.count_ge(mid) >= kk
            return jnp.where(ok, mid, lo), jnp.where(ok, hi, mid)

        lo, hi = lax.fori_loop(0, C_BISECT_STEPS, bisect, (lo, hi))
        thr = passes.max_where(lambda x: x <= hi)
        cnt = passes.count_ge(thr)

        def unsettled(c):
            return jnp.max(jnp.where(settled(c[1]), 0, 1)) > 0

        def walk(c):
            thr, cnt = c
            nxt = passes.max_where(lambda x: x < thr)
            ncnt = passes.count_ge(nxt)
            stay = settled(cnt)
            return jnp.where(stay, thr, nxt), jnp.where(stay, cnt, ncnt)

        return lax.while_loop(unsettled, walk, (thr, cnt))

    @pl.when(n_keys <= topk)
    def _():
        write_mask(lambda x, kpos: kpos >= 0)

    for n_chunks in range(1, seq // sel_tile + 1):
        @pl.when(jnp.logical_and(n_keys > topk, n_sel == n_chunks))
        def _(n_chunks=n_chunks):
            thr, cnt = kth_largest(_Passes(n_chunks))
            kth_ref[0:1, :] = thr
            kth_ref[1:2, :] = cnt

    @pl.when(n_keys > topk)
    def _():
        passes = _Passes(None)
        thr = jnp.where(searched, kth_ref[0:1, :], NEG_INF)
        cnt = kth_ref[1:2, :]
        has_ties = jnp.max(jnp.where(jnp.logical_and(searched, cnt > kk), 1, 0)) > 0

        @pl.when(jnp.logical_not(has_ties))
        def _():
            write_mask(lambda x, kpos: x >= thr)

        @pl.when(has_ties)
        def _():
            need = kk - passes.count(lambda x, k0: x > thr)
            cut = jnp.zeros((1, cq), jnp.int32)
            bit = seq
            while bit >= 1:
                cand = cut + bit
                before = passes.count(lambda x, k0: jnp.logical_and(x == thr, key_pos(k0, sel_tile) < cand))
                cut = jnp.where(before <= need, cand, cut)
                bit //= 2
            cut = jnp.where(searched, cut, 2 * seq)
            write_mask(lambda x, kpos: jnp.logical_or(x > thr, jnp.logical_and(x == thr, kpos < cut)))

    acc_ref[...] = jnp.zeros(acc_ref.shape, jnp.float32)

    def score_dots(t, slot):
        k0 = pl.multiple_of(t * tk, tk)
        for h in range(C_HEADS):
            s_ref[slot, h] = lax.dot_general(ck_ref[pl.ds(k0, tk), (h // 2) * LANES:(h // 2 + 1) * LANES],
                                             qz_ref[h], _NT, preferred_element_type=jnp.float32)

    def attend(t, slot, stats):
        mask = sc_ref[pl.ds(pl.multiple_of(t * tk, tk), tk), :]
        new_stats, weights = [], []
        for h in range(C_HEADS):
            m_old, l_old = stats[h]
            s = s_ref[slot, h] + mask
            m_new = jnp.maximum(m_old, jnp.max(fold(s, jnp.max), axis=0, keepdims=True))
            alpha = jnp.exp2(m_old - m_new)
            p = jnp.exp2(s - m_new)
            new_stats.append((m_new, alpha * l_old + jnp.sum(fold(p), axis=0, keepdims=True)))
            weights.append((alpha, p.astype(_MXU_DTYPE)))
        for h in range(C_HEADS):
            alpha, p = weights[h]
            acc_ref[h] = alpha * acc_ref[h] + jnp.dot(cvt_ref[t, h * HEAD_DIM:(h + 1) * HEAD_DIM, :], p,
                                                      preferred_element_type=jnp.float32)
        return tuple(new_stats)

    stats = tuple((jnp.full((1, cq), NEG_INF, jnp.float32), jnp.zeros((1, cq), jnp.float32))
                  for _ in range(C_HEADS))
    stats = _two_slot_pipeline(n_tiles, score_dots, attend, stats)
    out = jnp.concatenate([acc_ref[h] * (1.0 / stats[h][1]) for h in range(C_HEADS)], axis=0)
    o_ref[...] = out.T.astype(o_ref.dtype)


def _dsa_attention(ik2, iq, iwt, ck, cvt, cq, bsz, seq, key_tile):
    m = iq.shape[0]
    cqb = C_Q_BLOCK
    nq = seq // cqb
    topk = min(C_TOPK_MAX, seq // 4)
    sel_tile = min(seq, 512)
    assert key_tile == cqb and seq % sel_tile == 0 and sel_tile % key_tile == 0
    return pl.pallas_call(
        functools.partial(_dsa_kernel, topk=topk, key_tile=key_tile, sel_tile=sel_tile, seq=seq),
        grid=(bsz, nq),
        in_specs=[
            pl.BlockSpec((seq, LANES), lambda b, i: (b, 0)),
            pl.BlockSpec((cqb, IDX_HEADS * IDX_DIM), lambda b, i: (b * nq + i, 0)),
            pl.BlockSpec((None, IDX_HEADS, cqb), lambda b, i: (b, 0, i)),
            pl.BlockSpec((seq, C_WIDTH), lambda b, i: (b, 0)),
            pl.BlockSpec((None, seq // key_tile, C_WIDTH, key_tile), lambda b, i: (b, 0, 0, 0)),
            pl.BlockSpec((cqb, C_WIDTH), lambda b, i: (b * nq + i, 0)),
        ],
        out_specs=pl.BlockSpec((cqb, C_WIDTH), lambda b, i: (b * nq + i, 0)),
        out_shape=jax.ShapeDtypeStruct((m, C_WIDTH), _MXU_DTYPE),
        scratch_shapes=[
            pltpu.VMEM((seq, cqb), jnp.float32),
            pltpu.VMEM((IDX_HEADS, cqb, LANES), _MXU_DTYPE),
            pltpu.VMEM((C_HEADS, cqb, LANES), _MXU_DTYPE),
            pltpu.VMEM((C_HEADS, HEAD_DIM, cqb), jnp.float32),
            pltpu.VMEM((2, C_HEADS, key_tile, cqb), jnp.float32),
            pltpu.VMEM((2, IDX_HEADS, key_tile, cqb), jnp.float32),
            pltpu.VMEM((8, cqb), jnp.float32),
        ],
        compiler_params=_params(("parallel", "arbitrary")),
        name="dsa_attention",
    )(ik2, iq, iwt, ck, cvt, cq)


def _layer_norm(z, g, b):
    mu = jnp.mean(z, axis=-1, keepdims=True)
    zc = z - mu
    var = jnp.mean(zc * zc, axis=-1, keepdims=True)
    return zc * lax.rsqrt(var + NORM_EPS) * g + b


def _post_kernel(x_ref, oa_ref, ob_ref, oc_ref, p_ref, wo_ref, wup_ref, wdown_ref, wgate_ref, wple_ref,
                 g1_ref, b1_ref, g2_ref, b2_ref, y_ref, *, alpha, ff_tile):
    y = jnp.dot(oa_ref[...], wo_ref[0:A_WIDTH, :], preferred_element_type=jnp.float32)
    y = y + jnp.dot(ob_ref[...], wo_ref[A_WIDTH:A_WIDTH + B_WIDTH, :], preferred_element_type=jnp.float32)
    y = y + jnp.dot(oc_ref[...], wo_ref[A_WIDTH + B_WIDTH:, :], preferred_element_type=jnp.float32)
    emb = jnp.dot(p_ref[...].astype(_MXU_DTYPE), wple_ref[...], preferred_element_type=jnp.float32)
    x1 = _layer_norm(alpha * x_ref[...] + y, g1_ref[...], b1_ref[...])
    x1b = x1.astype(_MXU_DTYPE)
    ff = jnp.zeros(x1.shape, jnp.float32)
    for f0 in range(0, wup_ref.shape[1], ff_tile):
        u = jnp.dot(x1b, wup_ref[:, f0:f0 + ff_tile], preferred_element_type=jnp.float32)
        u = jnp.square(jnp.maximum(u, 0.0)).astype(_MXU_DTYPE)
        ff = ff + jnp.dot(u, wdown_ref[f0:f0 + ff_tile, :], preferred_element_type=jnp.float32)
    gate = jax.nn.sigmoid(jnp.dot(x1b, wgate_ref[...], preferred_element_type=jnp.float32))
    y_ref[...] = _layer_norm(alpha * x1 + ff + gate * emb, g2_ref[...], b2_ref[...])


def _post(x, oa, ob, oc, p_all, layer, wo, wup, wdown, wgate, wple, g1, b1, g2, b2, alpha, seq):
    m, d = x.shape
    tm = min(seq, 512)
    steps = m // tm

    def rows(width):
        return pl.BlockSpec((tm, width), lambda i: (i, 0))

    weights = [wo, wup, wdown, wgate, wple, g1, b1, g2, b2]
    return pl.pallas_call(
        functools.partial(_post_kernel, alpha=alpha, ff_tile=512),
        grid=(steps,),
        in_specs=[rows(d), rows(A_WIDTH), rows(B_WIDTH), rows(C_WIDTH),
                  pl.BlockSpec((tm, p_all.shape[1]), lambda i: (layer * steps + i, 0))]
                 + [_resident(t.shape) for t in weights],
        out_specs=rows(d),
        out_shape=jax.ShapeDtypeStruct((m, d), jnp.float32),
        compiler_params=_params(("parallel",)),
        name="post_mlp",
    )(x, oa, ob, oc, p_all, *weights)


def kernel(x, p, positions, w_in, rel_bias, lam_q1, lam_k1, lam_q2, lam_k2, diff_norm_g, w_o,
           ln1_g, ln1_b, w_up, w_down, w_ple_gate, w_ple, ln2_g, ln2_b):
    bsz, seq, d_model = x.shape
    depth = w_in.shape[0]
    m = bsz * seq
    assert seq % C_Q_BLOCK == 0 and w_in.shape[2] == IN_TOTAL
    key_tile = C_Q_BLOCK
    alpha = (2 * depth) ** 0.25
    cast = lambda t: t.astype(_MXU_DTYPE)

    cos, sin = _rope_tables(positions)
    xf = x.reshape(m, d_model)
    for i in range(depth):
        w = jnp.pad(cast(w_in[i]), ((0, 0), (0, IN_PADDED - IN_TOTAL)))
        a_qk, avt, bq, bk, bvt, cq, ck, cvt, iq, ik2, iwt = _in_proj(xf, w, cos, sin, bsz, seq, key_tile)
        lam_init = 0.8 - 0.6 * math.exp(-0.3 * i)
        lam_vecs = jnp.stack([lam_q1[i], lam_k1[i], lam_q2[i], lam_k2[i]]).astype(jnp.float32)
        o_a = _band_attention(a_qk, avt, _band_bias(rel_bias[i]), bsz, seq)
        o_b = _diff_attention(bq, bk, bvt, lam_vecs, diff_norm_g[i][None, :].astype(jnp.float32),
                              lam_init, bsz, seq)
        o_c = _dsa_attention(ik2, iq, iwt, ck, cvt, cq, bsz, seq, key_tile)
        row = lambda t: t[i][None, :].astype(jnp.float32)
        xf = _post(xf, o_a, o_b, o_c, p.reshape(depth * m, p.shape[-1]), i,
                   cast(w_o[i]), cast(w_up[i]), cast(w_down[i]), cast(w_ple_gate[i]), cast(w_ple[i]),
                   row(ln1_g), row(ln1_b), row(ln2_g), row(ln2_b), alpha, seq)
    return xf.reshape(bsz, seq, d_model)
```

```python
import functools
import math

import jax
import jax.numpy as jnp
import numpy as np
from jax import lax
from jax.experimental import pallas as pl
from jax.experimental.pallas import tpu as pltpu

CHUNK = 64
HEAD_DIM = 64
A_HEADS = 4
A_LEFT_CHUNKS = 8
REL_CLIP = 128
B_HEADS = 4
B_V_DIM = 2 * HEAD_DIM
C_HEADS = 4
C_TOPK_MAX = 256
IDX_HEADS = 8
IDX_DIM = 64
A_WIDTH = A_HEADS * HEAD_DIM
B_WIDTH = B_HEADS * B_V_DIM
C_WIDTH = C_HEADS * HEAD_DIM
ROPE_THETA = 10000.0
NORM_EPS = 1e-5
NEG_INF = -1e30
LOG2_E = math.log2(math.e)

LANES = 128
FOLD_ROWS = 64
Q_BLOCK = 128
B_Q_BLOCK = 512
B_KEY_TILE = 512
B_FOLD_ROWS = 16
C_FOLD_ROWS = 16
C_Q_BLOCK = 256
VMEM_LIMIT = 56 * 1024 * 1024

OFF_A = 0
OFF_BQ = 3 * A_WIDTH
OFF_BK = OFF_BQ + 2 * B_HEADS * HEAD_DIM
OFF_BV = OFF_BK + 2 * B_HEADS * HEAD_DIM
OFF_CQ = OFF_BV + B_WIDTH
OFF_CK = OFF_CQ + C_WIDTH
OFF_CV = OFF_CK + C_WIDTH
OFF_IQ = OFF_CV + C_WIDTH
OFF_IK = OFF_IQ + IDX_HEADS * IDX_DIM
IN_TOTAL = OFF_IK + IDX_DIM + IDX_HEADS
IN_PADDED = OFF_IK + LANES

_MXU_DTYPE = jnp.bfloat16
_NT = (((1,), (1,)), ((), ()))


def _params(sem, **flags):
    return pltpu.CompilerParams(dimension_semantics=sem, vmem_limit_bytes=VMEM_LIMIT, flags=flags or None)


def _resident(shape):
    return pl.BlockSpec(shape, lambda *_: (0,) * len(shape), pipeline_mode=pl.Buffered(1))


def _two_slot_pipeline(n_tiles, produce, consume, carry, consume_last=None):
    consume_last = consume_last or consume
    n_pairs = (n_tiles - 1) // 2
    produce(0, 0)

    def pair(j, c):
        t0 = 2 * j
        produce(t0 + 1, 1)
        c = consume(t0, 0, c)
        produce(t0 + 2, 0)
        return consume(t0 + 1, 1, c)

    carry = lax.fori_loop(0, n_pairs, pair, carry)
    t0 = 2 * n_pairs

    def one(c):
        return consume_last(t0, 0, c)

    def two(c):
        produce(t0 + 1, 1)
        return consume_last(t0 + 1, 1, consume(t0, 0, c))

    return lax.cond(n_tiles - t0 == 1, one, two, carry)


def _split_heads(t, even):
    zero = jnp.zeros((), t.dtype)
    return jnp.where(even, t, zero), jnp.where(even, zero, t)


def _rope_table_kernel(pos_ref, inv_ref, sgn_ref, cos_ref, sin_ref):
    ang = pos_ref[...].astype(jnp.float32) * inv_ref[...]
    cos_ref[...] = jnp.cos(ang)
    sin_ref[...] = jnp.sin(ang) * sgn_ref[...]


def _rope_tables(positions):
    m = positions.size
    tm = min(m, 2048)
    inv = ROPE_THETA ** (-jnp.arange(0, HEAD_DIM, 2, dtype=jnp.float32) / HEAD_DIM)
    inv = jnp.tile(inv, LANES // (HEAD_DIM // 2))[None, :]
    sgn = jnp.tile(jnp.concatenate([-jnp.ones(HEAD_DIM // 2), jnp.ones(HEAD_DIM // 2)]),
                   LANES // HEAD_DIM).astype(jnp.float32)[None, :]
    row = pl.BlockSpec((tm, LANES), lambda i: (i, 0))
    const = pl.BlockSpec((1, LANES), lambda i: (0, 0))
    return pl.pallas_call(
        _rope_table_kernel,
        grid=(m // tm,),
        in_specs=[pl.BlockSpec((tm, 1), lambda i: (i, 0)), const, const],
        out_specs=[row, row],
        out_shape=[jax.ShapeDtypeStruct((m, LANES), jnp.float32)] * 2,
        compiler_params=_params(("parallel",)),
        name="rope_tables",
    )(positions.reshape(m, 1), inv, sgn)


def _in_proj_kernel(x_ref, w_ref, cos_ref, sin_ref,
                    a_ref, avt_ref, bq_ref, bk_ref, bvt_ref, cq_ref, ck_ref, cvt_ref,
                    iq_ref, ik2_ref, iwt_ref, *, key_tile):
    tm = x_ref.shape[0]
    xb = x_ref[...].astype(_MXU_DTYPE)
    cos = cos_ref[...]
    sin = sin_ref[...]
    lane = lax.broadcasted_iota(jnp.int32, (tm, LANES), 1)
    low_half = (lane & (HEAD_DIM - 1)) < HEAD_DIM // 2
    qk_scale = HEAD_DIM ** -0.5 * LOG2_E

    def proj(c0, n):
        return jnp.dot(xb, w_ref[:, c0:c0 + n], preferred_element_type=jnp.float32)

    def rope(t):
        rot = jnp.where(low_half, pltpu.roll(t, LANES - HEAD_DIM // 2, 1),
                        pltpu.roll(t, HEAD_DIM // 2, 1))
        return t * cos + rot * sin

    def store(ref, h, roped=False, scale=None, col0=0):
        for j in range(h.shape[1] // LANES):
            t = h[:, j * LANES:(j + 1) * LANES]
            if roped:
                t = rope(t)
            if scale is not None:
                t = t * scale
            ref[:, col0 + j * LANES:col0 + (j + 1) * LANES] = t.astype(ref.dtype)

    store(a_ref, proj(OFF_A, A_WIDTH), scale=qk_scale)
    store(a_ref, proj(OFF_A + A_WIDTH, A_WIDTH), col0=A_WIDTH)
    av = proj(OFF_A + 2 * A_WIDTH, A_WIDTH)
    for c in range(tm // Q_BLOCK):
        avt_ref[c] = av[c * Q_BLOCK:(c + 1) * Q_BLOCK, :].T.astype(avt_ref.dtype)
    store(bq_ref, proj(OFF_BQ, OFF_BK - OFF_BQ), roped=True, scale=qk_scale)
    store(bk_ref, proj(OFF_BK, OFF_BV - OFF_BK), roped=True)
    bv = proj(OFF_BV, B_WIDTH)
    b_tile = bvt_ref.shape[-1]
    for c in range(tm // b_tile):
        bvt_ref[c] = bv[c * b_tile:(c + 1) * b_tile, :].T.astype(bvt_ref.dtype)
    store(cq_ref, proj(OFF_CQ, C_WIDTH), roped=True, scale=qk_scale)
    store(ck_ref, proj(OFF_CK, C_WIDTH), roped=True)
    cv = proj(OFF_CV, C_WIDTH)
    for c in range(tm // key_tile):
        cvt_ref[c] = cv[c * key_tile:(c + 1) * key_tile, :].T.astype(cvt_ref.dtype)
    store(iq_ref, proj(OFF_IQ, IDX_HEADS * IDX_DIM), roped=True)
    last = proj(OFF_IK, LANES)
    ikr = rope(last)
    ik2 = jnp.where(lane < IDX_DIM, ikr, pltpu.roll(ikr, IDX_DIM, 1))
    ik2_ref[...] = ik2.astype(ik2_ref.dtype)
    iwt_ref[...] = last.T[IDX_DIM:IDX_DIM + IDX_HEADS, :] * ((IDX_HEADS * IDX_DIM) ** -0.5)


def _in_proj(x, w, cos, sin, bsz, seq, key_tile):
    m, d = x.shape
    tm = min(seq, 1024)
    b_tile = min(seq, B_KEY_TILE)
    n_s = seq // tm
    f = _MXU_DTYPE

    def rows(width):
        return pl.BlockSpec((tm, width), lambda i: (i, 0))

    out_shape = [
        jax.ShapeDtypeStruct((m, 2 * A_WIDTH), f),
        jax.ShapeDtypeStruct((bsz, seq // Q_BLOCK, A_WIDTH, Q_BLOCK), f),
        jax.ShapeDtypeStruct((m, OFF_BK - OFF_BQ), f),
        jax.ShapeDtypeStruct((m, OFF_BV - OFF_BK), f),
        jax.ShapeDtypeStruct((bsz, seq // b_tile, B_WIDTH, b_tile), f),
        jax.ShapeDtypeStruct((m, C_WIDTH), f),
        jax.ShapeDtypeStruct((m, C_WIDTH), f),
        jax.ShapeDtypeStruct((bsz, seq // key_tile, C_WIDTH, key_tile), f),
        jax.ShapeDtypeStruct((m, IDX_HEADS * IDX_DIM), f),
        jax.ShapeDtypeStruct((m, LANES), f),
        jax.ShapeDtypeStruct((bsz, IDX_HEADS, seq), jnp.float32),
    ]
    out_specs = [
        rows(2 * A_WIDTH),
        pl.BlockSpec((None, tm // Q_BLOCK, A_WIDTH, Q_BLOCK), lambda i: (i // n_s, i % n_s, 0, 0)),
        rows(OFF_BK - OFF_BQ), rows(OFF_BV - OFF_BK),
        pl.BlockSpec((None, tm // b_tile, B_WIDTH, b_tile), lambda i: (i // n_s, i % n_s, 0, 0)),
        rows(C_WIDTH), rows(C_WIDTH),
        pl.BlockSpec((None, tm // key_tile, C_WIDTH, key_tile), lambda i: (i // n_s, i % n_s, 0, 0)),
        rows(IDX_HEADS * IDX_DIM), rows(LANES),
        pl.BlockSpec((None, IDX_HEADS, tm), lambda i: (i // n_s, 0, i % n_s)),
    ]
    return pl.pallas_call(
        functools.partial(_in_proj_kernel, key_tile=key_tile),
        grid=(m // tm,),
        in_specs=[rows(d), _resident(w.shape), rows(LANES), rows(LANES)],
        out_specs=out_specs,
        out_shape=out_shape,
        compiler_params=_params(("parallel",)),
        name="in_proj",
    )(x, w, cos, sin)


A_WINDOW_TILES = (A_LEFT_CHUNKS * CHUNK) // Q_BLOCK + 1
A_BLOCKS_PER_STEP = 4


def _band_bias(rel_bias):
    width = A_WINDOW_TILES * Q_BLOCK
    span = Q_BLOCK + width - 1
    d = np.arange(span) - (width - 1) + A_LEFT_CHUNKS * CHUNK
    gen = rel_bias[:, np.clip(d, -REL_CLIP, REL_CLIP) + REL_CLIP].astype(jnp.float32)
    sheared = jnp.tile(gen, (1, Q_BLOCK + 1))[:, :Q_BLOCK * (span + 1)].reshape(-1, Q_BLOCK, span + 1)
    bias = sheared[:, :, :width][:, :, ::-1]
    r = np.arange(Q_BLOCK)[:, None]
    e = np.arange(width)[None, :]
    back = r // CHUNK + A_LEFT_CHUNKS - e // CHUNK
    in_band = (back >= 0) & (back <= A_LEFT_CHUNKS)
    bias = jnp.where(jnp.asarray(in_band)[None], bias * LOG2_E, NEG_INF)
    return bias.reshape(A_HEADS // 2, 2, Q_BLOCK, width).transpose(0, 3, 1, 2).reshape(A_HEADS // 2, width, 2 * Q_BLOCK)


def _band_kernel(q_ref, k_ref, vt_ref, bias_ref, o_ref):
    even_head = lax.broadcasted_iota(jnp.int32, (Q_BLOCK, LANES), 1) < HEAD_DIM
    problems = []
    for sub in range(A_BLOCKS_PER_STEP):
        i = pl.program_id(1) * A_BLOCKS_PER_STEP + sub
        rows = slice(sub * Q_BLOCK, (sub + 1) * Q_BLOCK)
        tiles = []
        for tt in range(A_WINDOW_TILES):
            t = i - (A_WINDOW_TILES - 1) + tt
            tiles.append((t, jnp.maximum(t, 0)))
        for pair in range(A_HEADS // 2):
            problems.append((rows, slice(pair * LANES, (pair + 1) * LANES), pair, tiles))
    scores = []
    for rows, cols, pair, tiles in problems:
        qs = jnp.concatenate(_split_heads(q_ref[rows, cols], even_head), axis=0)
        scores.append([lax.dot_general(k_ref[pl.ds(pl.multiple_of(tc * Q_BLOCK, Q_BLOCK), Q_BLOCK), cols],
                                       qs, _NT, preferred_element_type=jnp.float32)
                       for t, tc in tiles])
    probs = []
    for (rows, cols, pair, tiles), s_raw in zip(problems, scores):
        s_tiles = [jnp.where(t >= 0, s + bias_ref[pair, tt * Q_BLOCK:(tt + 1) * Q_BLOCK, :], NEG_INF)
                   for tt, ((t, tc), s) in enumerate(zip(tiles, s_raw))]
        m = jnp.max(functools.reduce(jnp.maximum, s_tiles), axis=0, keepdims=True)
        p_tiles = [jnp.exp2(s - m) for s in s_tiles]
        l = jnp.sum(functools.reduce(jnp.add, p_tiles), axis=0, keepdims=True)
        probs.append(([p.astype(_MXU_DTYPE) for p in p_tiles], l))
    for (rows, cols, pair, tiles), (p_tiles, l) in zip(problems, probs):
        acc = jnp.zeros((LANES, 2 * Q_BLOCK), jnp.float32)
        for (t, tc), p in zip(tiles, p_tiles):
            acc = acc + jnp.dot(vt_ref[tc, cols, :], p, preferred_element_type=jnp.float32)
        acc = acc * (1.0 / l)
        out_t = jnp.concatenate([acc[:HEAD_DIM, :Q_BLOCK], acc[HEAD_DIM:, Q_BLOCK:]], axis=0)
        o_ref[rows, cols] = out_t.T.astype(o_ref.dtype)


def _band_attention(a_qk, avt, bias, bsz, seq):
    m = a_qk.shape[0]
    step_rows = A_BLOCKS_PER_STEP * Q_BLOCK
    nq = seq // step_rows
    return pl.pallas_call(
        _band_kernel,
        grid=(bsz, nq),
        in_specs=[
            pl.BlockSpec((step_rows, A_WIDTH), lambda b, i: (b * nq + i, 0)),
            pl.BlockSpec((seq, A_WIDTH), lambda b, i: (b, 1)),
            pl.BlockSpec((None, seq // Q_BLOCK, A_WIDTH, Q_BLOCK), lambda b, i: (b, 0, 0, 0)),
            _resident(bias.shape),
        ],
        out_specs=pl.BlockSpec((step_rows, A_WIDTH), lambda b, i: (b * nq + i, 0)),
        out_shape=jax.ShapeDtypeStruct((m, A_WIDTH), _MXU_DTYPE),
        compiler_params=_params(("parallel", "arbitrary")),
        name="band_attention",
    )(a_qk, a_qk, avt, bias)


def _diff_kernel(lam_ref, g_ref, q_ref, k_ref, vt_ref, o_ref, qs_ref, acc_ref, s_ref, p_ref,
                 *, lam_init, key_tile):
    i = pl.program_id(1)
    tk = key_tile
    bq = B_Q_BLOCK
    nq2 = 2 * bq
    lv = lam_ref[...]
    lam = (jnp.exp(jnp.sum(lv[0:1] * lv[1:2], axis=1, keepdims=True))
           - jnp.exp(jnp.sum(lv[2:3] * lv[3:4], axis=1, keepdims=True)) + lam_init)
    first_map = lax.broadcasted_iota(jnp.int32, (bq, LANES), 1) < HEAD_DIM
    for h in range(B_HEADS):
        qs_ref[h] = jnp.concatenate(_split_heads(q_ref[:, h * LANES:(h + 1) * LANES], first_map), axis=0)
    acc_ref[...] = jnp.zeros(acc_ref.shape, jnp.float32)
    col = lax.broadcasted_iota(jnp.int32, (1, nq2), 1)
    limit = i * bq + ((col & (bq - 1)) // CHUNK + 1) * CHUNK
    n_tiles = ((i + 1) * bq + tk - 1) // tk

    def fold(x, op):
        return op(x.reshape(tk // B_FOLD_ROWS, B_FOLD_ROWS, nq2), axis=0)

    def score_dots(t):
        k0 = pl.multiple_of(t * tk, tk)
        for h in range(B_HEADS):
            s_ref[h] = lax.dot_general(k_ref[pl.ds(k0, tk), h * LANES:(h + 1) * LANES], qs_ref[h], _NT,
                                       preferred_element_type=jnp.float32)

    def softmax_update(t, stats, masked):
        new_stats, weights = [], []
        for h in range(B_HEADS):
            s = s_ref[h]
            if masked:
                s = jnp.where(t * tk + lax.broadcasted_iota(jnp.int32, s.shape, 0) < limit, s, NEG_INF)
            m_old, l_old = stats[h]
            m_new = jnp.maximum(m_old, jnp.max(fold(s, jnp.max), axis=0, keepdims=True))
            alpha = jnp.exp2(m_old - m_new)
            p = jnp.exp2(s - m_new)
            new_stats.append((m_new, alpha * l_old + jnp.sum(fold(p, jnp.sum), axis=0, keepdims=True)))
            p_ref[h] = p.astype(_MXU_DTYPE)
            weights.append(alpha)
        return tuple(new_stats), weights

    def value_dots(t, weights):
        for h in range(B_HEADS):
            acc_ref[h] = weights[h] * acc_ref[h] + jnp.dot(vt_ref[t, h * B_V_DIM:(h + 1) * B_V_DIM, :], p_ref[h],
                                                      preferred_element_type=jnp.float32)

    def step(t, stats):
        stats, weights = softmax_update(t, stats, masked=False)
        score_dots(t + 1)
        value_dots(t, weights)
        return stats

    stats = tuple((jnp.full((1, nq2), NEG_INF, jnp.float32), jnp.zeros((1, nq2), jnp.float32))
                  for _ in range(B_HEADS))
    score_dots(0)
    stats = lax.fori_loop(0, n_tiles - 1, step, stats)
    stats, weights = softmax_update(n_tiles - 1, stats, masked=True)
    value_dots(n_tiles - 1, weights)

    for h in range(B_HEADS):
        acc = acc_ref[h] * (1.0 / stats[h][1])
        o = (acc[:, :bq] - acc[:, bq:] * lam).T
        ms = jnp.mean(o * o, axis=1, keepdims=True)
        o = o * lax.rsqrt(ms + NORM_EPS) * g_ref[...] * (1.0 - lam_init)
        o_ref[:, h * B_V_DIM:(h + 1) * B_V_DIM] = o.astype(o_ref.dtype)


def _diff_attention(bq, bk, bvt, lam_vecs, gain, lam_init, bsz, seq):
    m = bq.shape[0]
    nq = seq // B_Q_BLOCK
    n_kt, key_tile = bvt.shape[1], bvt.shape[3]
    return pl.pallas_call(
        functools.partial(_diff_kernel, lam_init=lam_init, key_tile=key_tile),
        grid=(bsz, nq),
        in_specs=[
            _resident(lam_vecs.shape),
            _resident(gain.shape),
            pl.BlockSpec((B_Q_BLOCK, B_HEADS * LANES), lambda b, i: (b * nq + i, 0)),
            pl.BlockSpec((seq, B_HEADS * LANES), lambda b, i: (b, 0)),
            pl.BlockSpec((None, n_kt, B_WIDTH, key_tile), lambda b, i: (b, 0, 0, 0)),
        ],
        out_specs=pl.BlockSpec((B_Q_BLOCK, B_WIDTH), lambda b, i: (b * nq + i, 0)),
        out_shape=jax.ShapeDtypeStruct((m, B_WIDTH), _MXU_DTYPE),
        scratch_shapes=[
            pltpu.VMEM((B_HEADS, 2 * B_Q_BLOCK, LANES), _MXU_DTYPE),
            pltpu.VMEM((B_HEADS, B_V_DIM, 2 * B_Q_BLOCK), jnp.float32),
            pltpu.VMEM((B_HEADS, key_tile, 2 * B_Q_BLOCK), jnp.float32),
            pltpu.VMEM((B_HEADS, key_tile, 2 * B_Q_BLOCK), _MXU_DTYPE),
        ],
        compiler_params=_params(("parallel", "arbitrary")),
        name="diff_attention",
    )(lam_vecs, gain, bq, bk, bvt)


C_BISECT_STEPS = 14


def _dsa_kernel(ik_ref, iq_ref, iw_ref, ck_ref, cvt_ref, cq_ref, o_ref,
                sc_ref, iqz_ref, qz_ref, acc_ref, s_ref, lg_ref, kth_ref, *, topk, key_tile, sel_tile, seq):
    i = pl.program_id(1)
    tk = key_tile
    cq = C_Q_BLOCK
    n_keys = (i + 1) * cq
    n_tiles = (n_keys + tk - 1) // tk
    n_sel = (n_keys + sel_tile - 1) // sel_tile
    lane_q = lax.broadcasted_iota(jnp.int32, (1, cq), 1)
    limit = i * cq + (lane_q // CHUNK + 1) * CHUNK
    searched = limit > topk
    even_head = lax.broadcasted_iota(jnp.int32, (cq, LANES), 1) < HEAD_DIM
    kk = float(topk)

    def fold(x, op=jnp.sum, rows=C_FOLD_ROWS):
        return op(x.reshape(x.shape[0] // rows, rows, cq), axis=0)

    def key_pos(k0, rows):
        return k0 + lax.broadcasted_iota(jnp.int32, (rows, cq), 0)

    for j in range(IDX_HEADS // 2):
        iqz_ref[2 * j], iqz_ref[2 * j + 1] = _split_heads(iq_ref[:, j * LANES:(j + 1) * LANES], even_head)
    for j in range(C_HEADS // 2):
        qz_ref[2 * j], qz_ref[2 * j + 1] = _split_heads(cq_ref[:, j * LANES:(j + 1) * LANES], even_head)
    w = iw_ref[...]

    def index_dots(t, slot):
        ikt = ik_ref[pl.ds(pl.multiple_of(t * tk, tk), tk), :]
        for h in range(IDX_HEADS):
            lg_ref[slot, h] = lax.dot_general(ikt, iqz_ref[h], _NT, preferred_element_type=jnp.float32)

    def index_combine(t, slot, c):
        k0 = pl.multiple_of(t * tk, tk)
        acc = jnp.zeros((tk, cq), jnp.float32)
        for h in range(IDX_HEADS):
            acc = acc + jnp.maximum(lg_ref[slot, h], 0.0) * w[h:h + 1, :]
        adm = key_pos(k0, tk) < limit
        sc_ref[pl.ds(k0, tk), :] = jnp.where(adm, acc, NEG_INF)
        return (jnp.minimum(c[0], fold(jnp.where(adm, acc, jnp.inf), jnp.min)),
                jnp.maximum(c[1], fold(jnp.where(adm, acc, NEG_INF), jnp.max)))

    lo8, hi8 = _two_slot_pipeline(n_tiles, index_dots, index_combine,
                                  (jnp.full((C_FOLD_ROWS, cq), jnp.inf, jnp.float32),
                                   jnp.full((C_FOLD_ROWS, cq), NEG_INF, jnp.float32)))

    @pl.when(n_tiles * tk < n_sel * sel_tile)
    def _():
        k0 = pl.multiple_of(n_tiles * tk, tk)
        sc_ref[pl.ds(k0, tk), :] = jnp.full((tk, cq), NEG_INF, jnp.float32)

    class _Passes:
        def __init__(self, n_chunks):
            self.n_chunks = n_chunks

        def over_scores(self, fn, init):
            if self.n_chunks is None:
                def body(t, acc):
                    k0 = pl.multiple_of(t * sel_tile, sel_tile)
                    return fn(sc_ref[pl.ds(k0, sel_tile), :], k0, acc)
                return lax.fori_loop(0, n_sel, body, init)
            acc = init
            for c in range(self.n_chunks):
                acc = fn(sc_ref[c * sel_tile:(c + 1) * sel_tile, :], c * sel_tile, acc)
            return acc

        def count(self, pred):
            def body(x, k0, acc):
                hit = pred(x, k0)
                for g in range(sel_tile // FOLD_ROWS):
                    acc = jnp.where(hit[g * FOLD_ROWS:(g + 1) * FOLD_ROWS], acc + 1.0, acc)
                return acc
            part = self.over_scores(body, jnp.zeros((FOLD_ROWS, cq), jnp.float32))
            return jnp.sum(part, axis=0, keepdims=True)

        def count_ge(self, thr):
            return self.count(lambda x, k0: x >= thr)

        def max_where(self, pred):
            part = self.over_scores(
                lambda x, k0, acc: jnp.maximum(acc, fold(jnp.where(pred(x), x, NEG_INF), jnp.max, FOLD_ROWS)),
                jnp.full((FOLD_ROWS, cq), NEG_INF, jnp.float32))
            return jnp.max(part, axis=0, keepdims=True)

    def write_mask(keep_fn):
        def body(t, carry):
            k0 = pl.multiple_of(t * tk, tk)
            x = sc_ref[pl.ds(k0, tk), :]
            kpos = key_pos(k0, tk)
            keep = jnp.logical_and(keep_fn(x, kpos), kpos < limit)
            sc_ref[pl.ds(k0, tk), :] = jnp.where(keep, 0.0, NEG_INF)
            return carry
        lax.fori_loop(0, n_tiles, body, 0)

    def settled(cnt):
        return jnp.logical_or(cnt >= kk, jnp.logical_not(searched))

    def kth_largest(passes):
        lo = jnp.min(lo8, axis=0, keepdims=True)
        hi = jnp.max(hi8, axis=0, keepdims=True)

        def bisect(_, c):
            lo, hi = c
            mid = lo + (hi - lo) * 0.5
            ok = passes.count_ge(mid) >= kk
            return jnp.where(ok, mid, lo), jnp.where(ok, hi, mid)

        lo, hi = lax.fori_loop(0, C_BISECT_STEPS, bisect, (lo, hi))
        thr = passes.max_where(lambda x: x <= hi)
        cnt = passes.count_ge(thr)

        def unsettled(c):
            return jnp.max(jnp.where(settled(c[1]), 0, 1)) > 0

        def walk(c):
            thr, cnt = c
            nxt = passes.max_where(lambda x: x < thr)
            ncnt = passes.count_ge(nxt)
            stay = settled(cnt)
            return jnp.where(stay, thr, nxt), jnp.where(stay, cnt, ncnt)

        return lax.while_loop(unsettled, walk, (thr, cnt))

    @pl.when(n_keys <= topk)
    def _():
        write_mask(lambda x, kpos: kpos >= 0)

    for n_chunks in range(1, seq // sel_tile + 1):
        @pl.when(jnp.logical_and(n_keys > topk, n_sel == n_chunks))
        def _(n_chunks=n_chunks):
            thr, cnt = kth_largest(_Passes(n_chunks))
            kth_ref[0:1, :] = thr
            kth_ref[1:2, :] = cnt

    @pl.when(n_keys > topk)
    def _():
        passes = _Passes(None)
        thr = jnp.where(searched, kth_ref[0:1, :], NEG_INF)
        cnt = kth_ref[1:2, :]
        has_ties = jnp.max(jnp.where(jnp.logical_and(searched, cnt > kk), 1, 0)) > 0

        @pl.when(jnp.logical_not(has_ties))
        def _():
            write_mask(lambda x, kpos: x >= thr)

        @pl.when(has_ties)
        def _():
            need = kk - passes.count(lambda x, k0: x > thr)
            cut = jnp.zeros((1, cq), jnp.int32)
            bit = seq
            while bit >= 1:
                cand = cut + bit
                before = passes.count(lambda x, k0: jnp.logical_and(x == thr, key_pos(k0, sel_tile) < cand))
                cut = jnp.where(before <= need, cand, cut)
                bit //= 2
            cut = jnp.where(searched, cut, 2 * seq)
            write_mask(lambda x, kpos: jnp.logical_or(x > thr, jnp.logical_and(x == thr, kpos < cut)))

    acc_ref[...] = jnp.zeros(acc_ref.shape, jnp.float32)

    def score_dots(t, slot):
        k0 = pl.multiple_of(t * tk, tk)
        for h in range(C_HEADS):
            s_ref[slot, h] = lax.dot_general(ck_ref[pl.ds(k0, tk), (h // 2) * LANES:(h // 2 + 1) * LANES],
                                             qz_ref[h], _NT, preferred_element_type=jnp.float32)

    def attend(t, slot, stats):
        mask = sc_ref[pl.ds(pl.multiple_of(t * tk, tk), tk), :]
        new_stats, weights = [], []
        for h in range(C_HEADS):
            m_old, l_old = stats[h]
            s = s_ref[slot, h] + mask
            m_new = jnp.maximum(m_old, jnp.max(fold(s, jnp.max), axis=0, keepdims=True))
            alpha = jnp.exp2(m_old - m_new)
            p = jnp.exp2(s - m_new)
            new_stats.append((m_new, alpha * l_old + jnp.sum(fold(p), axis=0, keepdims=True)))
            weights.append((alpha, p.astype(_MXU_DTYPE)))
        for h in range(C_HEADS):
            alpha, p = weights[h]
            acc_ref[h] = alpha * acc_ref[h] + jnp.dot(cvt_ref[t, h * HEAD_DIM:(h + 1) * HEAD_DIM, :], p,
                                                      preferred_element_type=jnp.float32)
        return tuple(new_stats)

    stats = tuple((jnp.full((1, cq), NEG_INF, jnp.float32), jnp.zeros((1, cq), jnp.float32))
                  for _ in range(C_HEADS))
    stats = _two_slot_pipeline(n_tiles, score_dots, attend, stats)
    out = jnp.concatenate([acc_ref[h] * (1.0 / stats[h][1]) for h in range(C_HEADS)], axis=0)
    o_ref[...] = out.T.astype(o_ref.dtype)


def _dsa_attention(ik2, iq, iwt, ck, cvt, cq, bsz, seq, key_tile):
    m = iq.shape[0]
    cqb = C_Q_BLOCK
    nq = seq // cqb
    topk = min(C_TOPK_MAX, seq // 4)
    sel_tile = min(seq, 512)
    assert key_tile == cqb and seq % sel_tile == 0 and sel_tile % key_tile == 0
    return pl.pallas_call(
        functools.partial(_dsa_kernel, topk=topk, key_tile=key_tile, sel_tile=sel_tile, seq=seq),
        grid=(bsz, nq),
        in_specs=[
            pl.BlockSpec((seq, LANES), lambda b, i: (b, 0)),
            pl.BlockSpec((cqb, IDX_HEADS * IDX_DIM), lambda b, i: (b * nq + i, 0)),
            pl.BlockSpec((None, IDX_HEADS, cqb), lambda b, i: (b, 0, i)),
            pl.BlockSpec((seq, C_WIDTH), lambda b, i: (b, 0)),
            pl.BlockSpec((None, seq // key_tile, C_WIDTH, key_tile), lambda b, i: (b, 0, 0, 0)),
            pl.BlockSpec((cqb, C_WIDTH), lambda b, i: (b * nq + i, 0)),
        ],
        out_specs=pl.BlockSpec((cqb, C_WIDTH), lambda b, i: (b * nq + i, 0)),
        out_shape=jax.ShapeDtypeStruct((m, C_WIDTH), _MXU_DTYPE),
        scratch_shapes=[
            pltpu.VMEM((seq, cqb), jnp.float32),
            pltpu.VMEM((IDX_HEADS, cqb, LANES), _MXU_DTYPE),
            pltpu.VMEM((C_HEADS, cqb, LANES), _MXU_DTYPE),
            pltpu.VMEM((C_HEADS, HEAD_DIM, cqb), jnp.float32),
            pltpu.VMEM((2, C_HEADS, key_tile, cqb), jnp.float32),
            pltpu.VMEM((2, IDX_HEADS, key_tile, cqb), jnp.float32),
            pltpu.VMEM((8, cqb), jnp.float32),
        ],
        compiler_params=_params(("parallel", "arbitrary")),
        name="dsa_attention",
    )(ik2, iq, iwt, ck, cvt, cq)


def _layer_norm(z, g, b):
    mu = jnp.mean(z, axis=-1, keepdims=True)
    zc = z - mu
    var = jnp.mean(zc * zc, axis=-1, keepdims=True)
    return zc * lax.rsqrt(var + NORM_EPS) * g + b


def _post_kernel(x_ref, oa_ref, ob_ref, oc_ref, p_ref, wo_ref, wup_ref, wdown_ref, wgate_ref, wple_ref,
                 g1_ref, b1_ref, g2_ref, b2_ref, y_ref, *, alpha, ff_tile):
    y = jnp.dot(oa_ref[...], wo_ref[0:A_WIDTH, :], preferred_element_type=jnp.float32)
    y = y + jnp.dot(ob_ref[...], wo_ref[A_WIDTH:A_WIDTH + B_WIDTH, :], preferred_element_type=jnp.float32)
    y = y + jnp.dot(oc_ref[...], wo_ref[A_WIDTH + B_WIDTH:, :], preferred_element_type=jnp.float32)
    emb = jnp.dot(p_ref[...].astype(_MXU_DTYPE), wple_ref[...], preferred_element_type=jnp.float32)
    x1 = _layer_norm(alpha * x_ref[...] + y, g1_ref[...], b1_ref[...])
    x1b = x1.astype(_MXU_DTYPE)
    ff = jnp.zeros(x1.shape, jnp.float32)
    for f0 in range(0, wup_ref.shape[1], ff_tile):
        u = jnp.dot(x1b, wup_ref[:, f0:f0 + ff_tile], preferred_element_type=jnp.float32)
        u = jnp.square(jnp.maximum(u, 0.0)).astype(_MXU_DTYPE)
        ff = ff + jnp.dot(u, wdown_ref[f0:f0 + ff_tile, :], preferred_element_type=jnp.float32)
    gate = jax.nn.sigmoid(jnp.dot(x1b, wgate_ref[...], preferred_element_type=jnp.float32))
    y_ref[...] = _layer_norm(alpha * x1 + ff + gate * emb, g2_ref[...], b2_ref[...])


def _post(x, oa, ob, oc, p_all, layer, wo, wup, wdown, wgate, wple, g1, b1, g2, b2, alpha, seq):
    m, d = x.shape
    tm = min(seq, 512)
    steps = m // tm

    def rows(width):
        return pl.BlockSpec((tm, width), lambda i: (i, 0))

    weights = [wo, wup, wdown, wgate, wple, g1, b1, g2, b2]
    return pl.pallas_call(
        functools.partial(_post_kernel, alpha=alpha, ff_tile=512),
        grid=(steps,),
        in_specs=[rows(d), rows(A_WIDTH), rows(B_WIDTH), rows(C_WIDTH),
                  pl.BlockSpec((tm, p_all.shape[1]), lambda i: (layer * steps + i, 0))]
                 + [_resident(t.shape) for t in weights],
        out_specs=rows(d),
        out_shape=jax.ShapeDtypeStruct((m, d), jnp.float32),
        compiler_params=_params(("parallel",)),
        name="post_mlp",
    )(x, oa, ob, oc, p_all, *weights)


def kernel(x, p, positions, w_in, rel_bias, lam_q1, lam_k1, lam_q2, lam_k2, diff_norm_g, w_o,
           ln1_g, ln1_b, w_up, w_down, w_ple_gate, w_ple, ln2_g, ln2_b):
    bsz, seq, d_model = x.shape
    depth = w_in.shape[0]
    m = bsz * seq
    assert seq % C_Q_BLOCK == 0 and w_in.shape[2] == IN_TOTAL
    key_tile = C_Q_BLOCK
    alpha = (2 * depth) ** 0.25
    cast = lambda t: t.astype(_MXU_DTYPE)

    cos, sin = _rope_tables(positions)
    xf = x.reshape(m, d_model)
    for i in range(depth):
        w = jnp.pad(cast(w_in[i]), ((0, 0), (0, IN_PADDED - IN_TOTAL)))
        a_qk, avt, bq, bk, bvt, cq, ck, cvt, iq, ik2, iwt = _in_proj(xf, w, cos, sin, bsz, seq, key_tile)
        lam_init = 0.8 - 0.6 * math.exp(-0.3 * i)
        lam_vecs = jnp.stack([lam_q1[i], lam_k1[i], lam_q2[i], lam_k2[i]]).astype(jnp.float32)
        o_a = _band_attention(a_qk, avt, _band_bias(rel_bias[i]), bsz, seq)
        o_b = _diff_attention(bq, bk, bvt, lam_vecs, diff_norm_g[i][None, :].astype(jnp.float32),
                              lam_init, bsz, seq)
        o_c = _dsa_attention(ik2, iq, iwt, ck, cvt, cq, bsz, seq, key_tile)
        row = lambda t: t[i][None, :].astype(jnp.float32)
        xf = _post(xf, o_a, o_b, o_c, p.reshape(depth * m, p.shape[-1]), i,
                   cast(w_o[i]), cast(w_up[i]), cast(w_down[i]), cast(w_ple_gate[i]), cast(w_ple[i]),
                   row(ln1_g), row(ln1_b), row(ln2_g), row(ln2_b), alpha, seq)
    return xf.reshape(bsz, seq, d_model)
```

```python
import functools
import math

import jax
import jax.numpy as jnp
import numpy as np
from jax import lax
from jax.experimental import pallas as pl
from jax.experimental.pallas import tpu as pltpu

CHUNK = 64
HEAD_DIM = 64
A_HEADS = 4
A_LEFT_CHUNKS = 8
REL_CLIP = 128
B_HEADS = 4
B_V_DIM = 2 * HEAD_DIM
C_HEADS = 4
C_TOPK_MAX = 256
IDX_HEADS = 8
IDX_DIM = 64
A_WIDTH = A_HEADS * HEAD_DIM
B_WIDTH = B_HEADS * B_V_DIM
C_WIDTH = C_HEADS * HEAD_DIM
ROPE_THETA = 10000.0
NORM_EPS = 1e-5
NEG_INF = -1e30
LOG2_E = math.log2(math.e)

LANES = 128
FOLD_ROWS = 64
Q_BLOCK = 128
B_Q_BLOCK = 512
B_KEY_TILE = 512
B_FOLD_ROWS = 16
C_FOLD_ROWS = 16
C_Q_BLOCK = 256
VMEM_LIMIT = 56 * 1024 * 1024

OFF_A = 0
OFF_BQ = 3 * A_WIDTH
OFF_BK = OFF_BQ + 2 * B_HEADS * HEAD_DIM
OFF_BV = OFF_BK + 2 * B_HEADS * HEAD_DIM
OFF_CQ = OFF_BV + B_WIDTH
OFF_CK = OFF_CQ + C_WIDTH
OFF_CV = OFF_CK + C_WIDTH
OFF_IQ = OFF_CV + C_WIDTH
OFF_IK = OFF_IQ + IDX_HEADS * IDX_DIM
IN_TOTAL = OFF_IK + IDX_DIM + IDX_HEADS
IN_PADDED = OFF_IK + LANES

_MXU_DTYPE = jnp.bfloat16
_NT = (((1,), (1,)), ((), ()))


def _params(sem, **flags):
    return pltpu.CompilerParams(dimension_semantics=sem, vmem_limit_bytes=VMEM_LIMIT, flags=flags or None)


def _resident(shape):
    return pl.BlockSpec(shape, lambda *_: (0,) * len(shape), pipeline_mode=pl.Buffered(1))


def _two_slot_pipeline(n_tiles, produce, consume, carry, consume_last=None):
    consume_last = consume_last or consume
    n_pairs = (n_tiles - 1) // 2
    produce(0, 0)

    def pair(j, c):
        t0 = 2 * j
        produce(t0 + 1, 1)
        c = consume(t0, 0, c)
        produce(t0 + 2, 0)
        return consume(t0 + 1, 1, c)

    carry = lax.fori_loop(0, n_pairs, pair, carry)
    t0 = 2 * n_pairs

    def one(c):
        return consume_last(t0, 0, c)

    def two(c):
        produce(t0 + 1, 1)
        return consume_last(t0 + 1, 1, consume(t0, 0, c))

    return lax.cond(n_tiles - t0 == 1, one, two, carry)


def _split_heads(t, even):
    zero = jnp.zeros((), t.dtype)
    return jnp.where(even, t, zero), jnp.where(even, zero, t)


def _rope_table_kernel(pos_ref, inv_ref, sgn_ref, cos_ref, sin_ref):
    ang = pos_ref[...].astype(jnp.float32) * inv_ref[...]
    cos_ref[...] = jnp.cos(ang)
    sin_ref[...] = jnp.sin(ang) * sgn_ref[...]


def _rope_tables(positions):
    m = positions.size
    tm = min(m, 2048)
    inv = ROPE_THETA ** (-jnp.arange(0, HEAD_DIM, 2, dtype=jnp.float32) / HEAD_DIM)
    inv = jnp.tile(inv, LANES // (HEAD_DIM // 2))[None, :]
    sgn = jnp.tile(jnp.concatenate([-jnp.ones(HEAD_DIM // 2), jnp.ones(HEAD_DIM // 2)]),
                   LANES // HEAD_DIM).astype(jnp.float32)[None, :]
    row = pl.BlockSpec((tm, LANES), lambda i: (i, 0))
    const = pl.BlockSpec((1, LANES), lambda i: (0, 0))
    return pl.pallas_call(
        _rope_table_kernel,
        grid=(m // tm,),
        in_specs=[pl.BlockSpec((tm, 1), lambda i: (i, 0)), const, const],
        out_specs=[row, row],
        out_shape=[jax.ShapeDtypeStruct((m, LANES), jnp.float32)] * 2,
        compiler_params=_params(("parallel",)),
        name="rope_tables",
    )(positions.reshape(m, 1), inv, sgn)


def _in_proj_kernel(x_ref, w_ref, cos_ref, sin_ref,
                    a_ref, avt_ref, bq_ref, bk_ref, bvt_ref, cq_ref, ck_ref, cvt_ref,
                    iq_ref, ik2_ref, iwt_ref, *, key_tile):
    tm = x_ref.shape[0]
    xb = x_ref[...].astype(_MXU_DTYPE)
    cos = cos_ref[...]
    sin = sin_ref[...]
    lane = lax.broadcasted_iota(jnp.int32, (tm, LANES), 1)
    low_half = (lane & (HEAD_DIM - 1)) < HEAD_DIM // 2
    qk_scale = HEAD_DIM ** -0.5 * LOG2_E

    def proj(c0, n):
        return jnp.dot(xb, w_ref[:, c0:c0 + n], preferred_element_type=jnp.float32)

    def rope(t):
        rot = jnp.where(low_half, pltpu.roll(t, LANES - HEAD_DIM // 2, 1),
                        pltpu.roll(t, HEAD_DIM // 2, 1))
        return t * cos + rot * sin

    def store(ref, h, roped=False, scale=None, col0=0):
        for j in range(h.shape[1] // LANES):
            t = h[:, j * LANES:(j + 1) * LANES]
            if roped:
                t = rope(t)
            if scale is not None:
                t = t * scale
            ref[:, col0 + j * LANES:col0 + (j + 1) * LANES] = t.astype(ref.dtype)

    store(a_ref, proj(OFF_A, A_WIDTH), scale=qk_scale)
    store(a_ref, proj(OFF_A + A_WIDTH, A_WIDTH), col0=A_WIDTH)
    av = proj(OFF_A + 2 * A_WIDTH, A_WIDTH)
    for c in range(tm // Q_BLOCK):
        avt_ref[c] = av[c * Q_BLOCK:(c + 1) * Q_BLOCK, :].T.astype(avt_ref.dtype)
    store(bq_ref, proj(OFF_BQ, OFF_BK - OFF_BQ), roped=True, scale=qk_scale)
    store(bk_ref, proj(OFF_BK, OFF_BV - OFF_BK), roped=True)
    bv = proj(OFF_BV, B_WIDTH)
    b_tile = bvt_ref.shape[-1]
    for c in range(tm // b_tile):
        bvt_ref[c] = bv[c * b_tile:(c + 1) * b_tile, :].T.astype(bvt_ref.dtype)
    store(cq_ref, proj(OFF_CQ, C_WIDTH), roped=True, scale=qk_scale)
    store(ck_ref, proj(OFF_CK, C_WIDTH), roped=True)
    cv = proj(OFF_CV, C_WIDTH)
    for c in range(tm // key_tile):
        cvt_ref[c] = cv[c * key_tile:(c + 1) * key_tile, :].T.astype(cvt_ref.dtype)
    store(iq_ref, proj(OFF_IQ, IDX_HEADS * IDX_DIM), roped=True)
    last = proj(OFF_IK, LANES)
    ikr = rope(last)
    ik2 = jnp.where(lane < IDX_DIM, ikr, pltpu.roll(ikr, IDX_DIM, 1))
    ik2_ref[...] = ik2.astype(ik2_ref.dtype)
    iwt_ref[...] = last.T[IDX_DIM:IDX_DIM + IDX_HEADS, :] * ((IDX_HEADS * IDX_DIM) ** -0.5)


def _in_proj(x, w, cos, sin, bsz, seq, key_tile):
    m, d = x.shape
    tm = min(seq, 1024)
    b_tile = min(seq, B_KEY_TILE)
    n_s = seq // tm
    f = _MXU_DTYPE

    def rows(width):
        return pl.BlockSpec((tm, width), lambda i: (i, 0))

    out_shape = [
        jax.ShapeDtypeStruct((m, 2 * A_WIDTH), f),
        jax.ShapeDtypeStruct((bsz, seq // Q_BLOCK, A_WIDTH, Q_BLOCK), f),
        jax.ShapeDtypeStruct((m, OFF_BK - OFF_BQ), f),
        jax.ShapeDtypeStruct((m, OFF_BV - OFF_BK), f),
        jax.ShapeDtypeStruct((bsz, seq // b_tile, B_WIDTH, b_tile), f),
        jax.ShapeDtypeStruct((m, C_WIDTH), f),
        jax.ShapeDtypeStruct((m, C_WIDTH), f),
        jax.ShapeDtypeStruct((bsz, seq // key_tile, C_WIDTH, key_tile), f),
        jax.ShapeDtypeStruct((m, IDX_HEADS * IDX_DIM), f),
        jax.ShapeDtypeStruct((m, LANES), f),
        jax.ShapeDtypeStruct((bsz, IDX_HEADS, seq), jnp.float32),
    ]
    out_specs = [
        rows(2 * A_WIDTH),
        pl.BlockSpec((None, tm // Q_BLOCK, A_WIDTH, Q_BLOCK), lambda i: (i // n_s, i % n_s, 0, 0)),
        rows(OFF_BK - OFF_BQ), rows(OFF_BV - OFF_BK),
        pl.BlockSpec((None, tm // b_tile, B_WIDTH, b_tile), lambda i: (i // n_s, i % n_s, 0, 0)),
        rows(C_WIDTH), rows(C_WIDTH),
        pl.BlockSpec((None, tm // key_tile, C_WIDTH, key_tile), lambda i: (i // n_s, i % n_s, 0, 0)),
        rows(IDX_HEADS * IDX_DIM), rows(LANES),
        pl.BlockSpec((None, IDX_HEADS, tm), lambda i: (i // n_s, 0, i % n_s)),
    ]
    return pl.pallas_call(
        functools.partial(_in_proj_kernel, key_tile=key_tile),
        grid=(m // tm,),
        in_specs=[rows(d), _resident(w.shape), rows(LANES), rows(LANES)],
        out_specs=out_specs,
        out_shape=out_shape,
        compiler_params=_params(("parallel",)),
        name="in_proj",
    )(x, w, cos, sin)


A_WINDOW_TILES = (A_LEFT_CHUNKS * CHUNK) // Q_BLOCK + 1
A_BLOCKS_PER_STEP = 4


def _band_bias(rel_bias):
    width = A_WINDOW_TILES * Q_BLOCK
    span = Q_BLOCK + width - 1
    d = np.arange(span) - (width - 1) + A_LEFT_CHUNKS * CHUNK
    gen = rel_bias[:, np.clip(d, -REL_CLIP, REL_CLIP) + REL_CLIP].astype(jnp.float32)
    sheared = jnp.tile(gen, (1, Q_BLOCK + 1))[:, :Q_BLOCK * (span + 1)].reshape(-1, Q_BLOCK, span + 1)
    bias = sheared[:, :, :width][:, :, ::-1]
    r = np.arange(Q_BLOCK)[:, None]
    e = np.arange(width)[None, :]
    back = r // CHUNK + A_LEFT_CHUNKS - e // CHUNK
    in_band = (back >= 0) & (back <= A_LEFT_CHUNKS)
    bias = jnp.where(jnp.asarray(in_band)[None], bias * LOG2_E, NEG_INF)
    return bias.reshape(A_HEADS // 2, 2, Q_BLOCK, width).transpose(0, 3, 1, 2).reshape(A_HEADS // 2, width, 2 * Q_BLOCK)


def _band_kernel(q_ref, k_ref, vt_ref, bias_ref, o_ref):
    even_head = lax.broadcasted_iota(jnp.int32, (Q_BLOCK, LANES), 1) < HEAD_DIM
    problems = []
    for sub in range(A_BLOCKS_PER_STEP):
        i = pl.program_id(1) * A_BLOCKS_PER_STEP + sub
        rows = slice(sub * Q_BLOCK, (sub + 1) * Q_BLOCK)
        tiles = []
        for tt in range(A_WINDOW_TILES):
            t = i - (A_WINDOW_TILES - 1) + tt
            tiles.append((t, jnp.maximum(t, 0)))
        for pair in range(A_HEADS // 2):
            problems.append((rows, slice(pair * LANES, (pair + 1) * LANES), pair, tiles))
    scores = []
    for rows, cols, pair, tiles in problems:
        qs = jnp.concatenate(_split_heads(q_ref[rows, cols], even_head), axis=0)
        scores.append([lax.dot_general(k_ref[pl.ds(pl.multiple_of(tc * Q_BLOCK, Q_BLOCK), Q_BLOCK), cols],
                                       qs, _NT, preferred_element_type=jnp.float32)
                       for t, tc in tiles])
    probs = []
    for (rows, cols, pair, tiles), s_raw in zip(problems, scores):
        s_tiles = [jnp.where(t >= 0, s + bias_ref[pair, tt * Q_BLOCK:(tt + 1) * Q_BLOCK, :], NEG_INF)
                   for tt, ((t, tc), s) in enumerate(zip(tiles, s_raw))]
        m = jnp.max(functools.reduce(jnp.maximum, s_tiles), axis=0, keepdims=True)
        p_tiles = [jnp.exp2(s - m) for s in s_tiles]
        l = jnp.sum(functools.reduce(jnp.add, p_tiles), axis=0, keepdims=True)
        probs.append(([p.astype(_MXU_DTYPE) for p in p_tiles], l))
    for (rows, cols, pair, tiles), (p_tiles, l) in zip(problems, probs):
        acc = jnp.zeros((LANES, 2 * Q_BLOCK), jnp.float32)
        for (t, tc), p in zip(tiles, p_tiles):
            acc = acc + jnp.dot(vt_ref[tc, cols, :], p, preferred_element_type=jnp.float32)
        acc = acc * (1.0 / l)
        out_t = jnp.concatenate([acc[:HEAD_DIM, :Q_BLOCK], acc[HEAD_DIM:, Q_BLOCK:]], axis=0)
        o_ref[rows, cols] = out_t.T.astype(o_ref.dtype)


def _band_attention(a_qk, avt, bias, bsz, seq):
    m = a_qk.shape[0]
    step_rows = A_BLOCKS_PER_STEP * Q_BLOCK
    nq = seq // step_rows
    return pl.pallas_call(
        _band_kernel,
        grid=(bsz, nq),
        in_specs=[
            pl.BlockSpec((step_rows, A_WIDTH), lambda b, i: (b * nq + i, 0)),
            pl.BlockSpec((seq, A_WIDTH), lambda b, i: (b, 1)),
            pl.BlockSpec((None, seq // Q_BLOCK, A_WIDTH, Q_BLOCK), lambda b, i: (b, 0, 0, 0)),
            _resident(bias.shape),
        ],
        out_specs=pl.BlockSpec((step_rows, A_WIDTH), lambda b, i: (b * nq + i, 0)),
        out_shape=jax.ShapeDtypeStruct((m, A_WIDTH), _MXU_DTYPE),
        compiler_params=_params(("parallel", "arbitrary")),
        name="band_attention",
    )(a_qk, a_qk, avt, bias)


def _diff_kernel(lam_ref, g_ref, q_ref, k_ref, vt_ref, o_ref, qs_ref, acc_ref, s_ref, p_ref,
                 *, lam_init, key_tile):
    i = pl.program_id(1)
    tk = key_tile
    bq = B_Q_BLOCK
    nq2 = 2 * bq
    lv = lam_ref[...]
    lam = (jnp.exp(jnp.sum(lv[0:1] * lv[1:2], axis=1, keepdims=True))
           - jnp.exp(jnp.sum(lv[2:3] * lv[3:4], axis=1, keepdims=True)) + lam_init)
    first_map = lax.broadcasted_iota(jnp.int32, (bq, LANES), 1) < HEAD_DIM
    for h in range(B_HEADS):
        qs_ref[h] = jnp.concatenate(_split_heads(q_ref[:, h * LANES:(h + 1) * LANES], first_map), axis=0)
    acc_ref[...] = jnp.zeros(acc_ref.shape, jnp.float32)
    col = lax.broadcasted_iota(jnp.int32, (1, nq2), 1)
    limit = i * bq + ((col & (bq - 1)) // CHUNK + 1) * CHUNK
    n_tiles = ((i + 1) * bq + tk - 1) // tk

    def fold(x, op):
        return op(x.reshape(tk // B_FOLD_ROWS, B_FOLD_ROWS, nq2), axis=0)

    def score_dots(t):
        k0 = pl.multiple_of(t * tk, tk)
        for h in range(B_HEADS):
            s_ref[h] = lax.dot_general(k_ref[pl.ds(k0, tk), h * LANES:(h + 1) * LANES], qs_ref[h], _NT,
                                       preferred_element_type=jnp.float32)

    def softmax_update(t, stats, masked):
        new_stats, weights = [], []
        for h in range(B_HEADS):
            s = s_ref[h]
            if masked:
                s = jnp.where(t * tk + lax.broadcasted_iota(jnp.int32, s.shape, 0) < limit, s, NEG_INF)
            m_old, l_old = stats[h]
            m_new = jnp.maximum(m_old, jnp.max(fold(s, jnp.max), axis=0, keepdims=True))
            alpha = jnp.exp2(m_old - m_new)
            p = jnp.exp2(s - m_new)
            new_stats.append((m_new, alpha * l_old + jnp.sum(fold(p, jnp.sum), axis=0, keepdims=True)))
            p_ref[h] = p.astype(_MXU_DTYPE)
            weights.append(alpha)
        return tuple(new_stats), weights

    def value_dots(t, weights):
        for h in range(B_HEADS):
            acc_ref[h] = weights[h] * acc_ref[h] + jnp.dot(vt_ref[t, h * B_V_DIM:(h + 1) * B_V_DIM, :], p_ref[h],
                                                      preferred_element_type=jnp.float32)

    def step(t, stats):
        stats, weights = softmax_update(t, stats, masked=False)
        score_dots(t + 1)
        value_dots(t, weights)
        return stats

    stats = tuple((jnp.full((1, nq2), NEG_INF, jnp.float32), jnp.zeros((1, nq2), jnp.float32))
                  for _ in range(B_HEADS))
    score_dots(0)
    stats = lax.fori_loop(0, n_tiles - 1, step, stats)
    stats, weights = softmax_update(n_tiles - 1, stats, masked=True)
    value_dots(n_tiles - 1, weights)

    for h in range(B_HEADS):
        acc = acc_ref[h] * (1.0 / stats[h][1])
        o = (acc[:, :bq] - acc[:, bq:] * lam).T
        ms = jnp.mean(o * o, axis=1, keepdims=True)
        o = o * lax.rsqrt(ms + NORM_EPS) * g_ref[...] * (1.0 - lam_init)
        o_ref[:, h * B_V_DIM:(h + 1) * B_V_DIM] = o.astype(o_ref.dtype)


def _diff_attention(bq, bk, bvt, lam_vecs, gain, lam_init, bsz, seq):
    m = bq.shape[0]
    nq = seq // B_Q_BLOCK
    n_kt, key_tile = bvt.shape[1], bvt.shape[3]
    return pl.pallas_call(
        functools.partial(_diff_kernel, lam_init=lam_init, key_tile=key_tile),
        grid=(bsz, nq),
        in_specs=[
            _resident(lam_vecs.shape),
            _resident(gain.shape),
            pl.BlockSpec((B_Q_BLOCK, B_HEADS * LANES), lambda b, i: (b * nq + i, 0)),
            pl.BlockSpec((seq, B_HEADS * LANES), lambda b, i: (b, 0)),
            pl.BlockSpec((None, n_kt, B_WIDTH, key_tile), lambda b, i: (b, 0, 0, 0)),
        ],
        out_specs=pl.BlockSpec((B_Q_BLOCK, B_WIDTH), lambda b, i: (b * nq + i, 0)),
        out_shape=jax.ShapeDtypeStruct((m, B_WIDTH), _MXU_DTYPE),
        scratch_shapes=[
            pltpu.VMEM((B_HEADS, 2 * B_Q_BLOCK, LANES), _MXU_DTYPE),
            pltpu.VMEM((B_HEADS, B_V_DIM, 2 * B_Q_BLOCK), jnp.float32),
            pltpu.VMEM((B_HEADS, key_tile, 2 * B_Q_BLOCK), jnp.float32),
            pltpu.VMEM((B_HEADS, key_tile, 2 * B_Q_BLOCK), _MXU_DTYPE),
        ],
        compiler_params=_params(("parallel", "arbitrary")),
        name="diff_attention",
    )(lam_vecs, gain, bq, bk, bvt)


C_BISECT_STEPS = 14


def _dsa_kernel(ik_ref, iq_ref, iw_ref, ck_ref, cvt_ref, cq_ref, o_ref,
                sc_ref, iqz_ref, qz_ref, acc_ref, s_ref, lg_ref, kth_ref, *, topk, key_tile, sel_tile, seq):
    i = pl.program_id(1)
    tk = key_tile
    cq = C_Q_BLOCK
    n_keys = (i + 1) * cq
    n_tiles = (n_keys + tk - 1) // tk
    n_sel = (n_keys + sel_tile - 1) // sel_tile
    lane_q = lax.broadcasted_iota(jnp.int32, (1, cq), 1)
    limit = i * cq + (lane_q // CHUNK + 1) * CHUNK
    searched = limit > topk
    even_head = lax.broadcasted_iota(jnp.int32, (cq, LANES), 1) < HEAD_DIM
    kk = float(topk)

    def fold(x, op=jnp.sum, rows=C_FOLD_ROWS):
        return op(x.reshape(x.shape[0] // rows, rows, cq), axis=0)

    def key_pos(k0, rows):
        return k0 + lax.broadcasted_iota(jnp.int32, (rows, cq), 0)

    for j in range(IDX_HEADS // 2):
        iqz_ref[2 * j], iqz_ref[2 * j + 1] = _split_heads(iq_ref[:, j * LANES:(j + 1) * LANES], even_head)
    for j in range(C_HEADS // 2):
        qz_ref[2 * j], qz_ref[2 * j + 1] = _split_heads(cq_ref[:, j * LANES:(j + 1) * LANES], even_head)
    w = iw_ref[...]

    def index_dots(t, slot):
        ikt = ik_ref[pl.ds(pl.multiple_of(t * tk, tk), tk), :]
        for h in range(IDX_HEADS):
            lg_ref[slot, h] = lax.dot_general(ikt, iqz_ref[h], _NT, preferred_element_type=jnp.float32)

    def index_combine(t, slot, c):
        k0 = pl.multiple_of(t * tk, tk)
        acc = jnp.zeros((tk, cq), jnp.float32)
        for h in range(IDX_HEADS):
            acc = acc + jnp.maximum(lg_ref[slot, h], 0.0) * w[h:h + 1, :]
        adm = key_pos(k0, tk) < limit
        sc_ref[pl.ds(k0, tk), :] = jnp.where(adm, acc, NEG_INF)
        return (jnp.minimum(c[0], fold(jnp.where(adm, acc, jnp.inf), jnp.min)),
                jnp.maximum(c[1], fold(jnp.where(adm, acc, NEG_INF), jnp.max)))

    lo8, hi8 = _two_slot_pipeline(n_tiles, index_dots, index_combine,
                                  (jnp.full((C_FOLD_ROWS, cq), jnp.inf, jnp.float32),
                                   jnp.full((C_FOLD_ROWS, cq), NEG_INF, jnp.float32)))

    @pl.when(n_tiles * tk < n_sel * sel_tile)
    def _():
        k0 = pl.multiple_of(n_tiles * tk, tk)
        sc_ref[pl.ds(k0, tk), :] = jnp.full((tk, cq), NEG_INF, jnp.float32)

    class _Passes:
        def __init__(self, n_chunks):
            self.n_chunks = n_chunks

        def over_scores(self, fn, init):
            if self.n_chunks is None:
                def body(t, acc):
                    k0 = pl.multiple_of(t * sel_tile, sel_tile)
                    return fn(sc_ref[pl.ds(k0, sel_tile), :], k0, acc)
                return lax.fori_loop(0, n_sel, body, init)
            acc = init
            for c in range(self.n_chunks):
                acc = fn(sc_ref[c * sel_tile:(c + 1) * sel_tile, :], c * sel_tile, acc)
            return acc

        def count(self, pred):
            def body(x, k0, acc):
                hit = pred(x, k0)
                for g in range(sel_tile // FOLD_ROWS):
                    acc = jnp.where(hit[g * FOLD_ROWS:(g + 1) * FOLD_ROWS], acc + 1.0, acc)
                return acc
            part = self.over_scores(body, jnp.zeros((FOLD_ROWS, cq), jnp.float32))
            return jnp.sum(part, axis=0, keepdims=True)

        def count_ge(self, thr):
            return self.count(lambda x, k0: x >= thr)

        def max_where(self, pred):
            part = self.over_scores(
                lambda x, k0, acc: jnp.maximum(acc, fold(jnp.where(pred(x), x, NEG_INF), jnp.max, FOLD_ROWS)),
                jnp.full((FOLD_ROWS, cq), NEG_INF, jnp.float32))
            return jnp.max(part, axis=0, keepdims=True)

    def write_mask(keep_fn):
        def body(t, carry):
            k0 = pl.multiple_of(t * tk, tk)
            x = sc_ref[pl.ds(k0, tk), :]
            kpos = key_pos(k0, tk)
            keep = jnp.logical_and(keep_fn(x, kpos), kpos < limit)
            sc_ref[pl.ds(k0, tk), :] = jnp.where(keep, 0.0, NEG_INF)
            return carry
        lax.fori_loop(0, n_tiles, body, 0)

    def settled(cnt):
        return jnp.logical_or(cnt >= kk, jnp.logical_not(searched))

    def kth_largest(passes):
        lo = jnp.min(lo8, axis=0, keepdims=True)
        hi = jnp.max(hi8, axis=0, keepdims=True)

        def bisect(_, c):
            lo, hi = c
            mid = lo + (hi - lo) * 0.5
            ok = passes.count_ge(mid) >= kk
            return jnp.where(ok, mid, lo), jnp.where(ok, hi, mid)

        lo, hi = lax.fori_loop(0, C_BISECT_STEPS, bisect, (lo, hi))
        thr = passes.max_where(lambda x: x <= hi)
        cnt = passes.count_ge(thr)

        def unsettled(c):
            return jnp.max(jnp.where(settled(c[1]), 0, 1)) > 0

        def walk(c):
            thr, cnt = c
            nxt = passes.max_where(lambda x: x < thr)
            ncnt = passes.count_ge(nxt)
            stay = settled(cnt)
            return jnp.where(stay, thr, nxt), jnp.where(stay, cnt, ncnt)

        return lax.while_loop(unsettled, walk, walk((thr, cnt)))

    @pl.when(n_keys <= topk)
    def _():
        write_mask(lambda x, kpos: kpos >= 0)

    for n_chunks in range(1, seq // sel_tile + 1):
        @pl.when(jnp.logical_and(n_keys > topk, n_sel == n_chunks))
        def _(n_chunks=n_chunks):
            thr, cnt = kth_largest(_Passes(n_chunks))
            kth_ref[0:1, :] = thr
            kth_ref[1:2, :] = cnt

    @pl.when(n_keys > topk)
    def _():
        passes = _Passes(None)
        thr = jnp.where(searched, kth_ref[0:1, :], NEG_INF)
        cnt = kth_ref[1:2, :]
        has_ties = jnp.max(jnp.where(jnp.logical_and(searched, cnt > kk), 1, 0)) > 0

        @pl.when(jnp.logical_not(has_ties))
        def _():
            write_mask(lambda x, kpos: x >= thr)

        @pl.when(has_ties)
        def _():
            need = kk - passes.count(lambda x, k0: x > thr)
            cut = jnp.zeros((1, cq), jnp.int32)
            bit = seq
            while bit >= 1:
                cand = cut + bit
                before = passes.count(lambda x, k0: jnp.logical_and(x == thr, key_pos(k0, sel_tile) < cand))
                cut = jnp.where(before <= need, cand, cut)
                bit //= 2
            cut = jnp.where(searched, cut, 2 * seq)
            write_mask(lambda x, kpos: jnp.logical_or(x > thr, jnp.logical_and(x == thr, kpos < cut)))

    acc_ref[...] = jnp.zeros(acc_ref.shape, jnp.float32)

    def score_dots(t, slot):
        k0 = pl.multiple_of(t * tk, tk)
        for h in range(C_HEADS):
            s_ref[slot, h] = lax.dot_general(ck_ref[pl.ds(k0, tk), (h // 2) * LANES:(h // 2 + 1) * LANES],
                                             qz_ref[h], _NT, preferred_element_type=jnp.float32)

    def attend(t, slot, stats):
        mask = sc_ref[pl.ds(pl.multiple_of(t * tk, tk), tk), :]
        new_stats, weights = [], []
        for h in range(C_HEADS):
            m_old, l_old = stats[h]
            s = s_ref[slot, h] + mask
            m_new = jnp.maximum(m_old, jnp.max(fold(s, jnp.max), axis=0, keepdims=True))
            alpha = jnp.exp2(m_old - m_new)
            p = jnp.exp2(s - m_new)
            new_stats.append((m_new, alpha * l_old + jnp.sum(fold(p), axis=0, keepdims=True)))
            weights.append((alpha, p.astype(_MXU_DTYPE)))
        for h in range(C_HEADS):
            alpha, p = weights[h]
            acc_ref[h] = alpha * acc_ref[h] + jnp.dot(cvt_ref[t, h * HEAD_DIM:(h + 1) * HEAD_DIM, :], p,
                                                      preferred_element_type=jnp.float32)
        return tuple(new_stats)

    stats = tuple((jnp.full((1, cq), NEG_INF, jnp.float32), jnp.zeros((1, cq), jnp.float32))
                  for _ in range(C_HEADS))
    stats = _two_slot_pipeline(n_tiles, score_dots, attend, stats)
    out = jnp.concatenate([acc_ref[h] * (1.0 / stats[h][1]) for h in range(C_HEADS)], axis=0)
    o_ref[...] = out.T.astype(o_ref.dtype)


def _dsa_attention(ik2, iq, iwt, ck, cvt, cq, bsz, seq, key_tile):
    m = iq.shape[0]
    cqb = C_Q_BLOCK
    nq = seq // cqb
    topk = min(C_TOPK_MAX, seq // 4)
    sel_tile = min(seq, 512)
    assert key_tile == cqb and seq % sel_tile == 0 and sel_tile % key_tile == 0
    return pl.pallas_call(
        functools.partial(_dsa_kernel, topk=topk, key_tile=key_tile, sel_tile=sel_tile, seq=seq),
        grid=(bsz, nq),
        in_specs=[
            pl.BlockSpec((seq, LANES), lambda b, i: (b, 0)),
            pl.BlockSpec((cqb, IDX_HEADS * IDX_DIM), lambda b, i: (b * nq + i, 0)),
            pl.BlockSpec((None, IDX_HEADS, cqb), lambda b, i: (b, 0, i)),
            pl.BlockSpec((seq, C_WIDTH), lambda b, i: (b, 0)),
            pl.BlockSpec((None, seq // key_tile, C_WIDTH, key_tile), lambda b, i: (b, 0, 0, 0)),
            pl.BlockSpec((cqb, C_WIDTH), lambda b, i: (b * nq + i, 0)),
        ],
        out_specs=pl.BlockSpec((cqb, C_WIDTH), lambda b, i: (b * nq + i, 0)),
        out_shape=jax.ShapeDtypeStruct((m, C_WIDTH), _MXU_DTYPE),
        scratch_shapes=[
            pltpu.VMEM((seq, cqb), jnp.float32),
            pltpu.VMEM((IDX_HEADS, cqb, LANES), _MXU_DTYPE),
            pltpu.VMEM((C_HEADS, cqb, LANES), _MXU_DTYPE),
            pltpu.VMEM((C_HEADS, HEAD_DIM, cqb), jnp.float32),
            pltpu.VMEM((2, C_HEADS, key_tile, cqb), jnp.float32),
            pltpu.VMEM((2, IDX_HEADS, key_tile, cqb), jnp.float32),
            pltpu.VMEM((8, cqb), jnp.float32),
        ],
        compiler_params=_params(("parallel", "arbitrary")),
        name="dsa_attention",
    )(ik2, iq, iwt, ck, cvt, cq)


def _layer_norm(z, g, b):
    mu = jnp.mean(z, axis=-1, keepdims=True)
    zc = z - mu
    var = jnp.mean(zc * zc, axis=-1, keepdims=True)
    return zc * lax.rsqrt(var + NORM_EPS) * g + b


def _post_kernel(x_ref, oa_ref, ob_ref, oc_ref, p_ref, wo_ref, wup_ref, wdown_ref, wgate_ref, wple_ref,
                 g1_ref, b1_ref, g2_ref, b2_ref, y_ref, *, alpha, ff_tile):
    y = jnp.dot(oa_ref[...], wo_ref[0:A_WIDTH, :], preferred_element_type=jnp.float32)
    y = y + jnp.dot(ob_ref[...], wo_ref[A_WIDTH:A_WIDTH + B_WIDTH, :], preferred_element_type=jnp.float32)
    y = y + jnp.dot(oc_ref[...], wo_ref[A_WIDTH + B_WIDTH:, :], preferred_element_type=jnp.float32)
    emb = jnp.dot(p_ref[...].astype(_MXU_DTYPE), wple_ref[...], preferred_element_type=jnp.float32)
    x1 = _layer_norm(alpha * x_ref[...] + y, g1_ref[...], b1_ref[...])
    x1b = x1.astype(_MXU_DTYPE)
    ff = jnp.zeros(x1.shape, jnp.float32)
    for f0 in range(0, wup_ref.shape[1], ff_tile):
        u = jnp.dot(x1b, wup_ref[:, f0:f0 + ff_tile], preferred_element_type=jnp.float32)
        u = jnp.square(jnp.maximum(u, 0.0)).astype(_MXU_DTYPE)
        ff = ff + jnp.dot(u, wdown_ref[f0:f0 + ff_tile, :], preferred_element_type=jnp.float32)
    gate = jax.nn.sigmoid(jnp.dot(x1b, wgate_ref[...], preferred_element_type=jnp.float32))
    y_ref[...] = _layer_norm(alpha * x1 + ff + gate * emb, g2_ref[...], b2_ref[...])


def _post(x, oa, ob, oc, p_all, layer, wo, wup, wdown, wgate, wple, g1, b1, g2, b2, alpha, seq):
    m, d = x.shape
    tm = min(seq, 512)
    steps = m // tm

    def rows(width):
        return pl.BlockSpec((tm, width), lambda i: (i, 0))

    weights = [wo, wup, wdown, wgate, wple, g1, b1, g2, b2]
    return pl.pallas_call(
        functools.partial(_post_kernel, alpha=alpha, ff_tile=512),
        grid=(steps,),
        in_specs=[rows(d), rows(A_WIDTH), rows(B_WIDTH), rows(C_WIDTH),
                  pl.BlockSpec((tm, p_all.shape[1]), lambda i: (layer * steps + i, 0))]
                 + [_resident(t.shape) for t in weights],
        out_specs=rows(d),
        out_shape=jax.ShapeDtypeStruct((m, d), jnp.float32),
        compiler_params=_params(("parallel",)),
        name="post_mlp",
    )(x, oa, ob, oc, p_all, *weights)


def kernel(x, p, positions, w_in, rel_bias, lam_q1, lam_k1, lam_q2, lam_k2, diff_norm_g, w_o,
           ln1_g, ln1_b, w_up, w_down, w_ple_gate, w_ple, ln2_g, ln2_b):
    bsz, seq, d_model = x.shape
    depth = w_in.shape[0]
    m = bsz * seq
    assert seq % C_Q_BLOCK == 0 and w_in.shape[2] == IN_TOTAL
    key_tile = C_Q_BLOCK
    alpha = (2 * depth) ** 0.25
    cast = lambda t: t.astype(_MXU_DTYPE)

    cos, sin = _rope_tables(positions)
    xf = x.reshape(m, d_model)
    for i in range(depth):
        w = jnp.pad(cast(w_in[i]), ((0, 0), (0, IN_PADDED - IN_TOTAL)))
        a_qk, avt, bq, bk, bvt, cq, ck, cvt, iq, ik2, iwt = _in_proj(xf, w, cos, sin, bsz, seq, key_tile)
        lam_init = 0.8 - 0.6 * math.exp(-0.3 * i)
        lam_vecs = jnp.stack([lam_q1[i], lam_k1[i], lam_q2[i], lam_k2[i]]).astype(jnp.float32)
        o_a = _band_attention(a_qk, avt, _band_bias(rel_bias[i]), bsz, seq)
        o_b = _diff_attention(bq, bk, bvt, lam_vecs, diff_norm_g[i][None, :].astype(jnp.float32),
                              lam_init, bsz, seq)
        o_c = _dsa_attention(ik2, iq, iwt, ck, cvt, cq, bsz, seq, key_tile)
        row = lambda t: t[i][None, :].astype(jnp.float32)
        xf = _post(xf, o_a, o_b, o_c, p.reshape(depth * m, p.shape[-1]), i,
                   cast(w_o[i]), cast(w_up[i]), cast(w_down[i]), cast(w_ple_gate[i]), cast(w_ple[i]),
                   row(ln1_g), row(ln1_b), row(ln2_g), row(ln2_b), alpha, seq)
    return xf.reshape(bsz, seq, d_model)
```

```python
import functools
import math

import jax
import jax.numpy as jnp
import numpy as np
from jax import lax
from jax.experimental import pallas as pl
from jax.experimental.pallas import tpu as pltpu

CHUNK = 64
HEAD_DIM = 64
A_HEADS = 4
A_LEFT_CHUNKS = 8
REL_CLIP = 128
B_HEADS = 4
B_V_DIM = 2 * HEAD_DIM
C_HEADS = 4
C_TOPK_MAX = 256
IDX_HEADS = 8
IDX_DIM = 64
A_WIDTH = A_HEADS * HEAD_DIM
B_WIDTH = B_HEADS * B_V_DIM
C_WIDTH = C_HEADS * HEAD_DIM
ROPE_THETA = 10000.0
NORM_EPS = 1e-5
NEG_INF = -1e30
LOG2_E = math.log2(math.e)

LANES = 128
FOLD_ROWS = 64
Q_BLOCK = 128
B_Q_BLOCK = 512
B_KEY_TILE = 512
B_FOLD_ROWS = 16
C_FOLD_ROWS = 16
C_Q_BLOCK = 256
VMEM_LIMIT = 56 * 1024 * 1024

OFF_A = 0
OFF_BQ = 3 * A_WIDTH
OFF_BK = OFF_BQ + 2 * B_HEADS * HEAD_DIM
OFF_BV = OFF_BK + 2 * B_HEADS * HEAD_DIM
OFF_CQ = OFF_BV + B_WIDTH
OFF_CK = OFF_CQ + C_WIDTH
OFF_CV = OFF_CK + C_WIDTH
OFF_IQ = OFF_CV + C_WIDTH
OFF_IK = OFF_IQ + IDX_HEADS * IDX_DIM
IN_TOTAL = OFF_IK + IDX_DIM + IDX_HEADS
IN_PADDED = OFF_IK + LANES

_MXU_DTYPE = jnp.bfloat16
_NT = (((1,), (1,)), ((), ()))


def _params(sem, **flags):
    return pltpu.CompilerParams(dimension_semantics=sem, vmem_limit_bytes=VMEM_LIMIT, flags=flags or None)


def _resident(shape):
    return pl.BlockSpec(shape, lambda *_: (0,) * len(shape), pipeline_mode=pl.Buffered(1))


def _two_slot_pipeline(n_tiles, produce, consume, carry, consume_last=None):
    consume_last = consume_last or consume
    n_pairs = (n_tiles - 1) // 2
    produce(0, 0)

    def pair(j, c):
        t0 = 2 * j
        produce(t0 + 1, 1)
        c = consume(t0, 0, c)
        produce(t0 + 2, 0)
        return consume(t0 + 1, 1, c)

    carry = lax.fori_loop(0, n_pairs, pair, carry)
    t0 = 2 * n_pairs

    def one(c):
        return consume_last(t0, 0, c)

    def two(c):
        produce(t0 + 1, 1)
        return consume_last(t0 + 1, 1, consume(t0, 0, c))

    return lax.cond(n_tiles - t0 == 1, one, two, carry)


def _split_heads(t, even):
    zero = jnp.zeros((), t.dtype)
    return jnp.where(even, t, zero), jnp.where(even, zero, t)


def _rope_table_kernel(pos_ref, inv_ref, sgn_ref, cos_ref, sin_ref):
    ang = pos_ref[...].astype(jnp.float32) * inv_ref[...]
    cos_ref[...] = jnp.cos(ang)
    sin_ref[...] = jnp.sin(ang) * sgn_ref[...]


def _rope_tables(positions):
    m = positions.size
    tm = min(m, 2048)
    inv = ROPE_THETA ** (-jnp.arange(0, HEAD_DIM, 2, dtype=jnp.float32) / HEAD_DIM)
    inv = jnp.tile(inv, LANES // (HEAD_DIM // 2))[None, :]
    sgn = jnp.tile(jnp.concatenate([-jnp.ones(HEAD_DIM // 2), jnp.ones(HEAD_DIM // 2)]),
                   LANES // HEAD_DIM).astype(jnp.float32)[None, :]
    row = pl.BlockSpec((tm, LANES), lambda i: (i, 0))
    const = pl.BlockSpec((1, LANES), lambda i: (0, 0))
    return pl.pallas_call(
        _rope_table_kernel,
        grid=(m // tm,),
        in_specs=[pl.BlockSpec((tm, 1), lambda i: (i, 0)), const, const],
        out_specs=[row, row],
        out_shape=[jax.ShapeDtypeStruct((m, LANES), jnp.float32)] * 2,
        compiler_params=_params(("parallel",)),
        name="rope_tables",
    )(positions.reshape(m, 1), inv, sgn)


def _in_proj_kernel(x_ref, w_ref, cos_ref, sin_ref,
                    a_ref, avt_ref, bq_ref, bk_ref, bvt_ref, cq_ref, ck_ref, cvt_ref,
                    iq_ref, ik2_ref, iwt_ref, *, key_tile):
    tm = x_ref.shape[0]
    xb = x_ref[...].astype(_MXU_DTYPE)
    cos = cos_ref[...]
    sin = sin_ref[...]
    lane = lax.broadcasted_iota(jnp.int32, (tm, LANES), 1)
    low_half = (lane & (HEAD_DIM - 1)) < HEAD_DIM // 2
    qk_scale = HEAD_DIM ** -0.5 * LOG2_E

    def proj(c0, n):
        return jnp.dot(xb, w_ref[:, c0:c0 + n], preferred_element_type=jnp.float32)

    def rope(t):
        rot = jnp.where(low_half, pltpu.roll(t, LANES - HEAD_DIM // 2, 1),
                        pltpu.roll(t, HEAD_DIM // 2, 1))
        return t * cos + rot * sin

    def store(ref, h, roped=False, scale=None, col0=0):
        for j in range(h.shape[1] // LANES):
            t = h[:, j * LANES:(j + 1) * LANES]
            if roped:
                t = rope(t)
            if scale is not None:
                t = t * scale
            ref[:, col0 + j * LANES:col0 + (j + 1) * LANES] = t.astype(ref.dtype)

    store(a_ref, proj(OFF_A, A_WIDTH), scale=qk_scale)
    store(a_ref, proj(OFF_A + A_WIDTH, A_WIDTH), col0=A_WIDTH)
    av = proj(OFF_A + 2 * A_WIDTH, A_WIDTH)
    for c in range(tm // Q_BLOCK):
        avt_ref[c] = av[c * Q_BLOCK:(c + 1) * Q_BLOCK, :].T.astype(avt_ref.dtype)
    store(bq_ref, proj(OFF_BQ, OFF_BK - OFF_BQ), roped=True, scale=qk_scale)
    store(bk_ref, proj(OFF_BK, OFF_BV - OFF_BK), roped=True)
    bv = proj(OFF_BV, B_WIDTH)
    b_tile = bvt_ref.shape[-1]
    for c in range(tm // b_tile):
        bvt_ref[c] = bv[c * b_tile:(c + 1) * b_tile, :].T.astype(bvt_ref.dtype)
    store(cq_ref, proj(OFF_CQ, C_WIDTH), roped=True, scale=qk_scale)
    store(ck_ref, proj(OFF_CK, C_WIDTH), roped=True)
    cv = proj(OFF_CV, C_WIDTH)
    for c in range(tm // key_tile):
        cvt_ref[c] = cv[c * key_tile:(c + 1) * key_tile, :].T.astype(cvt_ref.dtype)
    store(iq_ref, proj(OFF_IQ, IDX_HEADS * IDX_DIM), roped=True)
    last = proj(OFF_IK, LANES)
    ikr = rope(last)
    ik2 = jnp.where(lane < IDX_DIM, ikr, pltpu.roll(ikr, IDX_DIM, 1))
    ik2_ref[...] = ik2.astype(ik2_ref.dtype)
    iwt_ref[...] = last.T[IDX_DIM:IDX_DIM + IDX_HEADS, :] * ((IDX_HEADS * IDX_DIM) ** -0.5)


def _in_proj(x, w, cos, sin, bsz, seq, key_tile):
    m, d = x.shape
    tm = min(seq, 1024)
    b_tile = min(seq, B_KEY_TILE)
    n_s = seq // tm
    f = _MXU_DTYPE

    def rows(width):
        return pl.BlockSpec((tm, width), lambda i: (i, 0))

    out_shape = [
        jax.ShapeDtypeStruct((m, 2 * A_WIDTH), f),
        jax.ShapeDtypeStruct((bsz, seq // Q_BLOCK, A_WIDTH, Q_BLOCK), f),
        jax.ShapeDtypeStruct((m, OFF_BK - OFF_BQ), f),
        jax.ShapeDtypeStruct((m, OFF_BV - OFF_BK), f),
        jax.ShapeDtypeStruct((bsz, seq // b_tile, B_WIDTH, b_tile), f),
        jax.ShapeDtypeStruct((m, C_WIDTH), f),
        jax.ShapeDtypeStruct((m, C_WIDTH), f),
        jax.ShapeDtypeStruct((bsz, seq // key_tile, C_WIDTH, key_tile), f),
        jax.ShapeDtypeStruct((m, IDX_HEADS * IDX_DIM), f),
        jax.ShapeDtypeStruct((m, LANES), f),
        jax.ShapeDtypeStruct((bsz, IDX_HEADS, seq), jnp.float32),
    ]
    out_specs = [
        rows(2 * A_WIDTH),
        pl.BlockSpec((None, tm // Q_BLOCK, A_WIDTH, Q_BLOCK), lambda i: (i // n_s, i % n_s, 0, 0)),
        rows(OFF_BK - OFF_BQ), rows(OFF_BV - OFF_BK),
        pl.BlockSpec((None, tm // b_tile, B_WIDTH, b_tile), lambda i: (i // n_s, i % n_s, 0, 0)),
        rows(C_WIDTH), rows(C_WIDTH),
        pl.BlockSpec((None, tm // key_tile, C_WIDTH, key_tile), lambda i: (i // n_s, i % n_s, 0, 0)),
        rows(IDX_HEADS * IDX_DIM), rows(LANES),
        pl.BlockSpec((None, IDX_HEADS, tm), lambda i: (i // n_s, 0, i % n_s)),
    ]
    return pl.pallas_call(
        functools.partial(_in_proj_kernel, key_tile=key_tile),
        grid=(m // tm,),
        in_specs=[rows(d), _resident(w.shape), rows(LANES), rows(LANES)],
        out_specs=out_specs,
        out_shape=out_shape,
        compiler_params=_params(("parallel",)),
        name="in_proj",
    )(x, w, cos, sin)


A_WINDOW_TILES = (A_LEFT_CHUNKS * CHUNK) // Q_BLOCK + 1
A_BLOCKS_PER_STEP = 8


def _band_bias(rel_bias):
    width = A_WINDOW_TILES * Q_BLOCK
    span = Q_BLOCK + width - 1
    d = np.arange(span) - (width - 1) + A_LEFT_CHUNKS * CHUNK
    gen = rel_bias[:, np.clip(d, -REL_CLIP, REL_CLIP) + REL_CLIP].astype(jnp.float32)
    sheared = jnp.tile(gen, (1, Q_BLOCK + 1))[:, :Q_BLOCK * (span + 1)].reshape(-1, Q_BLOCK, span + 1)
    bias = sheared[:, :, :width][:, :, ::-1]
    r = np.arange(Q_BLOCK)[:, None]
    e = np.arange(width)[None, :]
    back = r // CHUNK + A_LEFT_CHUNKS - e // CHUNK
    in_band = (back >= 0) & (back <= A_LEFT_CHUNKS)
    bias = jnp.where(jnp.asarray(in_band)[None], bias * LOG2_E, NEG_INF)
    return bias.reshape(A_HEADS // 2, 2, Q_BLOCK, width).transpose(0, 3, 1, 2).reshape(A_HEADS // 2, width, 2 * Q_BLOCK)


def _band_kernel(q_ref, k_ref, vt_ref, bias_ref, o_ref):
    even_head = lax.broadcasted_iota(jnp.int32, (Q_BLOCK, LANES), 1) < HEAD_DIM
    problems = []
    for sub in range(A_BLOCKS_PER_STEP):
        i = pl.program_id(1) * A_BLOCKS_PER_STEP + sub
        rows = slice(sub * Q_BLOCK, (sub + 1) * Q_BLOCK)
        tiles = []
        for tt in range(A_WINDOW_TILES):
            t = i - (A_WINDOW_TILES - 1) + tt
            tiles.append((t, jnp.maximum(t, 0)))
        for pair in range(A_HEADS // 2):
            problems.append((rows, slice(pair * LANES, (pair + 1) * LANES), pair, tiles))
    scores = []
    for rows, cols, pair, tiles in problems:
        qs = jnp.concatenate(_split_heads(q_ref[rows, cols], even_head), axis=0)
        scores.append([lax.dot_general(k_ref[pl.ds(pl.multiple_of(tc * Q_BLOCK, Q_BLOCK), Q_BLOCK), cols],
                                       qs, _NT, preferred_element_type=jnp.float32)
                       for t, tc in tiles])
    probs = []
    for (rows, cols, pair, tiles), s_raw in zip(problems, scores):
        s_tiles = [jnp.where(t >= 0, s + bias_ref[pair, tt * Q_BLOCK:(tt + 1) * Q_BLOCK, :], NEG_INF)
                   for tt, ((t, tc), s) in enumerate(zip(tiles, s_raw))]
        m = jnp.max(functools.reduce(jnp.maximum, s_tiles), axis=0, keepdims=True)
        p_tiles = [jnp.exp2(s - m) for s in s_tiles]
        l = jnp.sum(functools.reduce(jnp.add, p_tiles), axis=0, keepdims=True)
        probs.append(([p.astype(_MXU_DTYPE) for p in p_tiles], l))
    for (rows, cols, pair, tiles), (p_tiles, l) in zip(problems, probs):
        acc = jnp.zeros((LANES, 2 * Q_BLOCK), jnp.float32)
        for (t, tc), p in zip(tiles, p_tiles):
            acc = acc + jnp.dot(vt_ref[tc, cols, :], p, preferred_element_type=jnp.float32)
        acc = acc * (1.0 / l)
        out_t = jnp.concatenate([acc[:HEAD_DIM, :Q_BLOCK], acc[HEAD_DIM:, Q_BLOCK:]], axis=0)
        o_ref[rows, cols] = out_t.T.astype(o_ref.dtype)


def _band_attention(a_qk, avt, bias, bsz, seq):
    m = a_qk.shape[0]
    step_rows = A_BLOCKS_PER_STEP * Q_BLOCK
    nq = seq // step_rows
    return pl.pallas_call(
        _band_kernel,
        grid=(bsz, nq),
        in_specs=[
            pl.BlockSpec((step_rows, A_WIDTH), lambda b, i: (b * nq + i, 0)),
            pl.BlockSpec((seq, A_WIDTH), lambda b, i: (b, 1)),
            pl.BlockSpec((None, seq // Q_BLOCK, A_WIDTH, Q_BLOCK), lambda b, i: (b, 0, 0, 0)),
            _resident(bias.shape),
        ],
        out_specs=pl.BlockSpec((step_rows, A_WIDTH), lambda b, i: (b * nq + i, 0)),
        out_shape=jax.ShapeDtypeStruct((m, A_WIDTH), _MXU_DTYPE),
        compiler_params=_params(("parallel", "arbitrary")),
        name="band_attention",
    )(a_qk, a_qk, avt, bias)


def _diff_kernel(lam_ref, g_ref, q_ref, k_ref, vt_ref, o_ref, qs_ref, acc_ref, s_ref, p_ref,
                 *, lam_init, key_tile):
    i = pl.program_id(1)
    tk = key_tile
    bq = B_Q_BLOCK
    nq2 = 2 * bq
    lv = lam_ref[...]
    lam = (jnp.exp(jnp.sum(lv[0:1] * lv[1:2], axis=1, keepdims=True))
           - jnp.exp(jnp.sum(lv[2:3] * lv[3:4], axis=1, keepdims=True)) + lam_init)
    first_map = lax.broadcasted_iota(jnp.int32, (bq, LANES), 1) < HEAD_DIM
    for h in range(B_HEADS):
        qs_ref[h] = jnp.concatenate(_split_heads(q_ref[:, h * LANES:(h + 1) * LANES], first_map), axis=0)
    acc_ref[...] = jnp.zeros(acc_ref.shape, jnp.float32)
    col = lax.broadcasted_iota(jnp.int32, (1, nq2), 1)
    limit = i * bq + ((col & (bq - 1)) // CHUNK + 1) * CHUNK
    n_tiles = ((i + 1) * bq + tk - 1) // tk

    def fold(x, op):
        return op(x.reshape(tk // B_FOLD_ROWS, B_FOLD_ROWS, nq2), axis=0)

    def score_dots(t):
        k0 = pl.multiple_of(t * tk, tk)
        for h in range(B_HEADS):
            s_ref[h] = lax.dot_general(k_ref[pl.ds(k0, tk), h * LANES:(h + 1) * LANES], qs_ref[h], _NT,
                                       preferred_element_type=jnp.float32)

    def softmax_update(t, stats, masked):
        new_stats, weights = [], []
        for h in range(B_HEADS):
            s = s_ref[h]
            if masked:
                s = jnp.where(t * tk + lax.broadcasted_iota(jnp.int32, s.shape, 0) < limit, s, NEG_INF)
            m_old, l_old = stats[h]
            m_new = jnp.maximum(m_old, jnp.max(fold(s, jnp.max), axis=0, keepdims=True))
            alpha = jnp.exp2(m_old - m_new)
            p = jnp.exp2(s - m_new)
            new_stats.append((m_new, alpha * l_old + jnp.sum(fold(p, jnp.sum), axis=0, keepdims=True)))
            p_ref[h] = p.astype(_MXU_DTYPE)
            weights.append(alpha)
        return tuple(new_stats), weights

    def value_dots(t, weights):
        for h in range(B_HEADS):
            acc_ref[h] = weights[h] * acc_ref[h] + jnp.dot(vt_ref[t, h * B_V_DIM:(h + 1) * B_V_DIM, :], p_ref[h],
                                                      preferred_element_type=jnp.float32)

    def step(t, stats):
        stats, weights = softmax_update(t, stats, masked=False)
        score_dots(t + 1)
        value_dots(t, weights)
        return stats

    stats = tuple((jnp.full((1, nq2), NEG_INF, jnp.float32), jnp.zeros((1, nq2), jnp.float32))
                  for _ in range(B_HEADS))
    score_dots(0)
    stats = lax.fori_loop(0, n_tiles - 1, step, stats)
    stats, weights = softmax_update(n_tiles - 1, stats, masked=True)
    value_dots(n_tiles - 1, weights)

    for h in range(B_HEADS):
        acc = acc_ref[h] * (1.0 / stats[h][1])
        o = (acc[:, :bq] - acc[:, bq:] * lam).T
        ms = jnp.mean(o * o, axis=1, keepdims=True)
        o = o * lax.rsqrt(ms + NORM_EPS) * g_ref[...] * (1.0 - lam_init)
        o_ref[:, h * B_V_DIM:(h + 1) * B_V_DIM] = o.astype(o_ref.dtype)


def _diff_attention(bq, bk, bvt, lam_vecs, gain, lam_init, bsz, seq):
    m = bq.shape[0]
    nq = seq // B_Q_BLOCK
    n_kt, key_tile = bvt.shape[1], bvt.shape[3]
    return pl.pallas_call(
        functools.partial(_diff_kernel, lam_init=lam_init, key_tile=key_tile),
        grid=(bsz, nq),
        in_specs=[
            _resident(lam_vecs.shape),
            _resident(gain.shape),
            pl.BlockSpec((B_Q_BLOCK, B_HEADS * LANES), lambda b, i: (b * nq + i, 0)),
            pl.BlockSpec((seq, B_HEADS * LANES), lambda b, i: (b, 0)),
            pl.BlockSpec((None, n_kt, B_WIDTH, key_tile), lambda b, i: (b, 0, 0, 0)),
        ],
        out_specs=pl.BlockSpec((B_Q_BLOCK, B_WIDTH), lambda b, i: (b * nq + i, 0)),
        out_shape=jax.ShapeDtypeStruct((m, B_WIDTH), _MXU_DTYPE),
        scratch_shapes=[
            pltpu.VMEM((B_HEADS, 2 * B_Q_BLOCK, LANES), _MXU_DTYPE),
            pltpu.VMEM((B_HEADS, B_V_DIM, 2 * B_Q_BLOCK), jnp.float32),
            pltpu.VMEM((B_HEADS, key_tile, 2 * B_Q_BLOCK), jnp.float32),
            pltpu.VMEM((B_HEADS, key_tile, 2 * B_Q_BLOCK), _MXU_DTYPE),
        ],
        compiler_params=_params(("parallel", "arbitrary")),
        name="diff_attention",
    )(lam_vecs, gain, bq, bk, bvt)


C_BISECT_STEPS = 14


def _dsa_kernel(ik_ref, iq_ref, iw_ref, ck_ref, cvt_ref, cq_ref, o_ref,
                sc_ref, iqz_ref, qz_ref, acc_ref, s_ref, lg_ref, kth_ref, *, topk, key_tile, sel_tile, seq):
    i = pl.program_id(1)
    tk = key_tile
    cq = C_Q_BLOCK
    n_keys = (i + 1) * cq
    n_tiles = (n_keys + tk - 1) // tk
    n_sel = (n_keys + sel_tile - 1) // sel_tile
    lane_q = lax.broadcasted_iota(jnp.int32, (1, cq), 1)
    limit = i * cq + (lane_q // CHUNK + 1) * CHUNK
    searched = limit > topk
    even_head = lax.broadcasted_iota(jnp.int32, (cq, LANES), 1) < HEAD_DIM
    kk = float(topk)

    def fold(x, op=jnp.sum, rows=C_FOLD_ROWS):
        return op(x.reshape(x.shape[0] // rows, rows, cq), axis=0)

    def key_pos(k0, rows):
        return k0 + lax.broadcasted_iota(jnp.int32, (rows, cq), 0)

    for j in range(IDX_HEADS // 2):
        iqz_ref[2 * j], iqz_ref[2 * j + 1] = _split_heads(iq_ref[:, j * LANES:(j + 1) * LANES], even_head)
    for j in range(C_HEADS // 2):
        qz_ref[2 * j], qz_ref[2 * j + 1] = _split_heads(cq_ref[:, j * LANES:(j + 1) * LANES], even_head)
    w = iw_ref[...]

    def index_dots(t, slot):
        ikt = ik_ref[pl.ds(pl.multiple_of(t * tk, tk), tk), :]
        for h in range(IDX_HEADS):
            lg_ref[slot, h] = lax.dot_general(ikt, iqz_ref[h], _NT, preferred_element_type=jnp.float32)

    def index_combine(t, slot, c):
        k0 = pl.multiple_of(t * tk, tk)
        acc = jnp.zeros((tk, cq), jnp.float32)
        for h in range(IDX_HEADS):
            acc = acc + jnp.maximum(lg_ref[slot, h], 0.0) * w[h:h + 1, :]
        adm = key_pos(k0, tk) < limit
        sc_ref[pl.ds(k0, tk), :] = jnp.where(adm, acc, NEG_INF)
        return (jnp.minimum(c[0], fold(jnp.where(adm, acc, jnp.inf), jnp.min)),
                jnp.maximum(c[1], fold(jnp.where(adm, acc, NEG_INF), jnp.max)))

    lo8, hi8 = _two_slot_pipeline(n_tiles, index_dots, index_combine,
                                  (jnp.full((C_FOLD_ROWS, cq), jnp.inf, jnp.float32),
                                   jnp.full((C_FOLD_ROWS, cq), NEG_INF, jnp.float32)))

    @pl.when(n_tiles * tk < n_sel * sel_tile)
    def _():
        k0 = pl.multiple_of(n_tiles * tk, tk)
        sc_ref[pl.ds(k0, tk), :] = jnp.full((tk, cq), NEG_INF, jnp.float32)

    class _Passes:
        def __init__(self, n_chunks):
            self.n_chunks = n_chunks

        def over_scores(self, fn, init):
            if self.n_chunks is None:
                def body(t, acc):
                    k0 = pl.multiple_of(t * sel_tile, sel_tile)
                    return fn(sc_ref[pl.ds(k0, sel_tile), :], k0, acc)
                return lax.fori_loop(0, n_sel, body, init)
            acc = init
            for c in range(self.n_chunks):
                acc = fn(sc_ref[c * sel_tile:(c + 1) * sel_tile, :], c * sel_tile, acc)
            return acc

        def count(self, pred):
            def body(x, k0, acc):
                hit = pred(x, k0)
                for g in range(sel_tile // FOLD_ROWS):
                    acc = jnp.where(hit[g * FOLD_ROWS:(g + 1) * FOLD_ROWS], acc + 1.0, acc)
                return acc
            part = self.over_scores(body, jnp.zeros((FOLD_ROWS, cq), jnp.float32))
            return jnp.sum(part, axis=0, keepdims=True)

        def count_ge(self, thr):
            return self.count(lambda x, k0: x >= thr)

        def max_where(self, pred):
            part = self.over_scores(
                lambda x, k0, acc: jnp.maximum(acc, fold(jnp.where(pred(x), x, NEG_INF), jnp.max, FOLD_ROWS)),
                jnp.full((FOLD_ROWS, cq), NEG_INF, jnp.float32))
            return jnp.max(part, axis=0, keepdims=True)

    def write_mask(keep_fn):
        def body(t, carry):
            k0 = pl.multiple_of(t * tk, tk)
            x = sc_ref[pl.ds(k0, tk), :]
            kpos = key_pos(k0, tk)
            keep = jnp.logical_and(keep_fn(x, kpos), kpos < limit)
            sc_ref[pl.ds(k0, tk), :] = jnp.where(keep, 0.0, NEG_INF)
            return carry
        lax.fori_loop(0, n_tiles, body, 0)

    def settled(cnt):
        return jnp.logical_or(cnt >= kk, jnp.logical_not(searched))

    def kth_largest(passes):
        lo = jnp.min(lo8, axis=0, keepdims=True)
        hi = jnp.max(hi8, axis=0, keepdims=True)

        def bisect(_, c):
            lo, hi = c
            mid = lo + (hi - lo) * 0.5
            ok = passes.count_ge(mid) >= kk
            return jnp.where(ok, mid, lo), jnp.where(ok, hi, mid)

        lo, hi = lax.fori_loop(0, C_BISECT_STEPS, bisect, (lo, hi))
        thr = passes.max_where(lambda x: x <= hi)
        cnt = passes.count_ge(thr)

        def unsettled(c):
            return jnp.max(jnp.where(settled(c[1]), 0, 1)) > 0

        def walk(c):
            thr, cnt = c
            nxt = passes.max_where(lambda x: x < thr)
            ncnt = passes.count_ge(nxt)
            stay = settled(cnt)
            return jnp.where(stay, thr, nxt), jnp.where(stay, cnt, ncnt)

        return lax.while_loop(unsettled, walk, walk((thr, cnt)))

    @pl.when(n_keys <= topk)
    def _():
        write_mask(lambda x, kpos: kpos >= 0)

    for n_chunks in range(1, seq // sel_tile + 1):
        @pl.when(jnp.logical_and(n_keys > topk, n_sel == n_chunks))
        def _(n_chunks=n_chunks):
            thr, cnt = kth_largest(_Passes(n_chunks))
            kth_ref[0:1, :] = thr
            kth_ref[1:2, :] = cnt

    @pl.when(n_keys > topk)
    def _():
        passes = _Passes(None)
        thr = jnp.where(searched, kth_ref[0:1, :], NEG_INF)
        cnt = kth_ref[1:2, :]
        has_ties = jnp.max(jnp.where(jnp.logical_and(searched, cnt > kk), 1, 0)) > 0

        @pl.when(jnp.logical_not(has_ties))
        def _():
            write_mask(lambda x, kpos: x >= thr)

        @pl.when(has_ties)
        def _():
            need = kk - passes.count(lambda x, k0: x > thr)
            cut = jnp.zeros((1, cq), jnp.int32)
            bit = seq
            while bit >= 1:
                cand = cut + bit
                before = passes.count(lambda x, k0: jnp.logical_and(x == thr, key_pos(k0, sel_tile) < cand))
                cut = jnp.where(before <= need, cand, cut)
                bit //= 2
            cut = jnp.where(searched, cut, 2 * seq)
            write_mask(lambda x, kpos: jnp.logical_or(x > thr, jnp.logical_and(x == thr, kpos < cut)))

    acc_ref[...] = jnp.zeros(acc_ref.shape, jnp.float32)

    def score_dots(t, slot):
        k0 = pl.multiple_of(t * tk, tk)
        for h in range(C_HEADS):
            s_ref[slot, h] = lax.dot_general(ck_ref[pl.ds(k0, tk), (h // 2) * LANES:(h // 2 + 1) * LANES],
                                             qz_ref[h], _NT, preferred_element_type=jnp.float32)

    def attend(t, slot, stats):
        mask = sc_ref[pl.ds(pl.multiple_of(t * tk, tk), tk), :]
        new_stats, weights = [], []
        for h in range(C_HEADS):
            m_old, l_old = stats[h]
            s = s_ref[slot, h] + mask
            m_new = jnp.maximum(m_old, jnp.max(fold(s, jnp.max), axis=0, keepdims=True))
            alpha = jnp.exp2(m_old - m_new)
            p = jnp.exp2(s - m_new)
            new_stats.append((m_new, alpha * l_old + jnp.sum(fold(p), axis=0, keepdims=True)))
            weights.append((alpha, p.astype(_MXU_DTYPE)))
        for h in range(C_HEADS):
            alpha, p = weights[h]
            acc_ref[h] = alpha * acc_ref[h] + jnp.dot(cvt_ref[t, h * HEAD_DIM:(h + 1) * HEAD_DIM, :], p,
                                                      preferred_element_type=jnp.float32)
        return tuple(new_stats)

    stats = tuple((jnp.full((1, cq), NEG_INF, jnp.float32), jnp.zeros((1, cq), jnp.float32))
                  for _ in range(C_HEADS))
    stats = _two_slot_pipeline(n_tiles, score_dots, attend, stats)
    out = jnp.concatenate([acc_ref[h] * (1.0 / stats[h][1]) for h in range(C_HEADS)], axis=0)
    o_ref[...] = out.T.astype(o_ref.dtype)


def _dsa_attention(ik2, iq, iwt, ck, cvt, cq, bsz, seq, key_tile):
    m = iq.shape[0]
    cqb = C_Q_BLOCK
    nq = seq // cqb
    topk = min(C_TOPK_MAX, seq // 4)
    sel_tile = min(seq, 512)
    assert key_tile == cqb and seq % sel_tile == 0 and sel_tile % key_tile == 0
    return pl.pallas_call(
        functools.partial(_dsa_kernel, topk=topk, key_tile=key_tile, sel_tile=sel_tile, seq=seq),
        grid=(bsz, nq),
        in_specs=[
            pl.BlockSpec((seq, LANES), lambda b, i: (b, 0)),
            pl.BlockSpec((cqb, IDX_HEADS * IDX_DIM), lambda b, i: (b * nq + i, 0)),
            pl.BlockSpec((None, IDX_HEADS, cqb), lambda b, i: (b, 0, i)),
            pl.BlockSpec((seq, C_WIDTH), lambda b, i: (b, 0)),
            pl.BlockSpec((None, seq // key_tile, C_WIDTH, key_tile), lambda b, i: (b, 0, 0, 0)),
            pl.BlockSpec((cqb, C_WIDTH), lambda b, i: (b * nq + i, 0)),
        ],
        out_specs=pl.BlockSpec((cqb, C_WIDTH), lambda b, i: (b * nq + i, 0)),
        out_shape=jax.ShapeDtypeStruct((m, C_WIDTH), _MXU_DTYPE),
        scratch_shapes=[
            pltpu.VMEM((seq, cqb), jnp.float32),
            pltpu.VMEM((IDX_HEADS, cqb, LANES), _MXU_DTYPE),
            pltpu.VMEM((C_HEADS, cqb, LANES), _MXU_DTYPE),
            pltpu.VMEM((C_HEADS, HEAD_DIM, cqb), jnp.float32),
            pltpu.VMEM((2, C_HEADS, key_tile, cqb), jnp.float32),
            pltpu.VMEM((2, IDX_HEADS, key_tile, cqb), jnp.float32),
            pltpu.VMEM((8, cqb), jnp.float32),
        ],
        compiler_params=_params(("parallel", "arbitrary")),
        name="dsa_attention",
    )(ik2, iq, iwt, ck, cvt, cq)


def _layer_norm(z, g, b):
    mu = jnp.mean(z, axis=-1, keepdims=True)
    zc = z - mu
    var = jnp.mean(zc * zc, axis=-1, keepdims=True)
    return zc * lax.rsqrt(var + NORM_EPS) * g + b


def _post_kernel(x_ref, oa_ref, ob_ref, oc_ref, p_ref, wo_ref, wup_ref, wdown_ref, wgate_ref, wple_ref,
                 g1_ref, b1_ref, g2_ref, b2_ref, y_ref, *, alpha, ff_tile):
    y = jnp.dot(oa_ref[...], wo_ref[0:A_WIDTH, :], preferred_element_type=jnp.float32)
    y = y + jnp.dot(ob_ref[...], wo_ref[A_WIDTH:A_WIDTH + B_WIDTH, :], preferred_element_type=jnp.float32)
    y = y + jnp.dot(oc_ref[...], wo_ref[A_WIDTH + B_WIDTH:, :], preferred_element_type=jnp.float32)
    emb = jnp.dot(p_ref[...].astype(_MXU_DTYPE), wple_ref[...], preferred_element_type=jnp.float32)
    x1 = _layer_norm(alpha * x_ref[...] + y, g1_ref[...], b1_ref[...])
    x1b = x1.astype(_MXU_DTYPE)
    ff = jnp.zeros(x1.shape, jnp.float32)
    for f0 in range(0, wup_ref.shape[1], ff_tile):
        u = jnp.dot(x1b, wup_ref[:, f0:f0 + ff_tile], preferred_element_type=jnp.float32)
        u = jnp.square(jnp.maximum(u, 0.0)).astype(_MXU_DTYPE)
        ff = ff + jnp.dot(u, wdown_ref[f0:f0 + ff_tile, :], preferred_element_type=jnp.float32)
    gate = jax.nn.sigmoid(jnp.dot(x1b, wgate_ref[...], preferred_element_type=jnp.float32))
    y_ref[...] = _layer_norm(alpha * x1 + ff + gate * emb, g2_ref[...], b2_ref[...])


def _post(x, oa, ob, oc, p_all, layer, wo, wup, wdown, wgate, wple, g1, b1, g2, b2, alpha, seq):
    m, d = x.shape
    tm = min(seq, 512)
    steps = m // tm

    def rows(width):
        return pl.BlockSpec((tm, width), lambda i: (i, 0))

    weights = [wo, wup, wdown, wgate, wple, g1, b1, g2, b2]
    return pl.pallas_call(
        functools.partial(_post_kernel, alpha=alpha, ff_tile=512),
        grid=(steps,),
        in_specs=[rows(d), rows(A_WIDTH), rows(B_WIDTH), rows(C_WIDTH),
                  pl.BlockSpec((tm, p_all.shape[1]), lambda i: (layer * steps + i, 0))]
                 + [_resident(t.shape) for t in weights],
        out_specs=rows(d),
        out_shape=jax.ShapeDtypeStruct((m, d), jnp.float32),
        compiler_params=_params(("parallel",)),
        name="post_mlp",
    )(x, oa, ob, oc, p_all, *weights)


def kernel(x, p, positions, w_in, rel_bias, lam_q1, lam_k1, lam_q2, lam_k2, diff_norm_g, w_o,
           ln1_g, ln1_b, w_up, w_down, w_ple_gate, w_ple, ln2_g, ln2_b):
    bsz, seq, d_model = x.shape
    depth = w_in.shape[0]
    m = bsz * seq
    assert seq % C_Q_BLOCK == 0 and w_in.shape[2] == IN_TOTAL
    key_tile = C_Q_BLOCK
    alpha = (2 * depth) ** 0.25
    cast = lambda t: t.astype(_MXU_DTYPE)

    cos, sin = _rope_tables(positions)
    xf = x.reshape(m, d_model)
    for i in range(depth):
        w = jnp.pad(cast(w_in[i]), ((0, 0), (0, IN_PADDED - IN_TOTAL)))
        a_qk, avt, bq, bk, bvt, cq, ck, cvt, iq, ik2, iwt = _in_proj(xf, w, cos, sin, bsz, seq, key_tile)
        lam_init = 0.8 - 0.6 * math.exp(-0.3 * i)
        lam_vecs = jnp.stack([lam_q1[i], lam_k1[i], lam_q2[i], lam_k2[i]]).astype(jnp.float32)
        o_a = _band_attention(a_qk, avt, _band_bias(rel_bias[i]), bsz, seq)
        o_b = _diff_attention(bq, bk, bvt, lam_vecs, diff_norm_g[i][None, :].astype(jnp.float32),
                              lam_init, bsz, seq)
        o_c = _dsa_attention(ik2, iq, iwt, ck, cvt, cq, bsz, seq, key_tile)
        row = lambda t: t[i][None, :].astype(jnp.float32)
        xf = _post(xf, o_a, o_b, o_c, p.reshape(depth * m, p.shape[-1]), i,
                   cast(w_o[i]), cast(w_up[i]), cast(w_down[i]), cast(w_ple_gate[i]), cast(w_ple[i]),
                   row(ln1_g), row(ln1_b), row(ln2_g), row(ln2_b), alpha, seq)
    return xf.reshape(bsz, seq, d_model)
```
